```python
import math
import jax, jax.numpy as jnp
from jax import lax
import numpy as np

D_MODEL = 1024
BATCH = 4
SEQ = 8192
DEPTH = 4

N_A_LAYERS = DEPTH // 2
N_B_LAYERS = DEPTH - N_A_LAYERS

GDN_HEADS = 8
GDN_HEAD_DIM = D_MODEL // GDN_HEADS
GDN_WIDTH = GDN_HEADS * GDN_HEAD_DIM
CONV_K = 4
GDN_CHUNK = 64
A_IN_COLS = 4 * GDN_WIDTH + 2 * GDN_HEADS

MOBA_HEADS = 8
MOBA_HEAD_DIM = D_MODEL // MOBA_HEADS
MOBA_WIDTH = MOBA_HEADS * MOBA_HEAD_DIM
MOBA_BLOCK = 256
MOBA_TOP_K = 3
MOBA_Q_CHUNK = 32

DEEPNORM_ALPHA = (2 * DEPTH) ** 0.25
DEEPNORM_BETA = (8 * DEPTH) ** -0.25
LN_EPS = 1e-5
RMS_EPS = 1e-6

kernel_name = "yoco_gated_deltanet_moba_deepnorm"


def layer_norm(x, g, b):
    xf = x.astype(jnp.float32)
    mu = jnp.mean(xf, axis=-1, keepdims=True)
    var = jnp.mean(jnp.square(xf - mu), axis=-1, keepdims=True)
    y = (xf - mu) * lax.rsqrt(var + LN_EPS) * g.astype(jnp.float32) + b.astype(jnp.float32)
    return y.astype(x.dtype)


def l2_normalize(t):
    return t * lax.rsqrt(jnp.sum(jnp.square(t), axis=-1, keepdims=True) + RMS_EPS)


def causal_depthwise_conv(x, w):
    k_w, c = w.shape
    return lax.conv_general_dilated(
        x, w[:, None, :].astype(x.dtype), window_strides=(1,), padding=[(k_w - 1, 0)],
        dimension_numbers=("NWC", "WIO", "NWC"), feature_group_count=c)


def gated_delta_rule_chunked(q, k, v, g, beta):
    B, T, H, dk = q.shape
    dv = v.shape[-1]
    C = GDN_CHUNK
    N = T // C

    def chunks(t):
        return t.reshape(B, N, C, H, -1).transpose(1, 0, 3, 2, 4)

    q = chunks(q) * (dk ** -0.5)
    k = chunks(k)
    v = chunks(v)
    g = chunks(g[..., None])[..., 0]
    beta = chunks(beta[..., None])[..., 0]
    gc = jnp.cumsum(g, axis=-1)
    pos = jnp.arange(C)
    causal = pos[:, None] >= pos[None, :]
    strict = pos[:, None] > pos[None, :]
    decay = jnp.exp(jnp.where(causal, gc[..., :, None] - gc[..., None, :], -jnp.inf))
    k_beta = k * beta[..., None]
    lower = jnp.where(strict, jnp.einsum('nbhid,nbhjd->nbhij', k_beta, k) * decay, 0.0)
    a_mat = jnp.eye(C, dtype=jnp.float32) + lower
    w = lax.linalg.triangular_solve(a_mat, k_beta * jnp.exp(gc)[..., None],
                                    left_side=True, lower=True, unit_diagonal=True)
    u = lax.linalg.triangular_solve(a_mat, v * beta[..., None],
                                    left_side=True, lower=True, unit_diagonal=True)
    attn = jnp.einsum('nbhid,nbhjd->nbhij', q, k) * decay
    q_dec = q * jnp.exp(gc)[..., None]
    k_dec = k * jnp.exp(gc[..., -1:] - gc)[..., None]
    chunk_decay = jnp.exp(gc[..., -1])

    def step(state, inp):
        qd, wc, uc, ac, kd, cd = inp
        v_new = uc - jnp.einsum('bhcd,bhde->bhce', wc, state)
        o = jnp.einsum('bhcd,bhde->bhce', qd, state) + jnp.einsum('bhij,bhje->bhie', ac, v_new)
        state = state * cd[..., None, None] + jnp.einsum('bhcd,bhce->bhde', kd, v_new)
        return state, o

    state0 = jnp.zeros((B, H, dk, dv), jnp.float32)
    _, o = lax.scan(step, state0, (q_dec, w, u, attn, k_dec, chunk_decay))
    return o.transpose(1, 0, 3, 2, 4).reshape(B, T, H, dv)


def gated_deltanet_mixer(h, w_in, conv_w, a_log, dt_bias, norm_w, w_out):
    B, T, _ = h.shape
    proj = h @ w_in
    qkv, z, b, a = jnp.split(proj, [3 * GDN_WIDTH, 4 * GDN_WIDTH, 4 * GDN_WIDTH + GDN_HEADS], axis=-1)
    qkv = jax.nn.silu(causal_depthwise_conv(qkv, conv_w)).astype(jnp.float32)
    q, k, v = jnp.split(qkv, 3, axis=-1)
    q = l2_normalize(q.reshape(B, T, GDN_HEADS, GDN_HEAD_DIM))
    k = l2_normalize(k.reshape(B, T, GDN_HEADS, GDN_HEAD_DIM))
    v = v.reshape(B, T, GDN_HEADS, GDN_HEAD_DIM)
    beta = jax.nn.sigmoid(b.astype(jnp.float32))
    g = -jnp.exp(a_log.astype(jnp.float32)) * jax.nn.softplus(
        a.astype(jnp.float32) + dt_bias.astype(jnp.float32))
    o = gated_delta_rule_chunked(q, k, v, g, beta)
    o = o * lax.rsqrt(jnp.mean(jnp.square(o), axis=-1, keepdims=True) + RMS_EPS) * norm_w.astype(jnp.float32)
    o = o * jax.nn.silu(z.astype(jnp.float32)).reshape(B, T, GDN_HEADS, GDN_HEAD_DIM)
    return o.reshape(B, T, GDN_WIDTH).astype(h.dtype) @ w_out


def shared_kv(h, w_kv):
    B, T, _ = h.shape
    nb = -(-T // MOBA_BLOCK)
    t_pad = nb * MOBA_BLOCK
    kv = (h @ w_kv).reshape(B, T, 2, MOBA_HEADS, MOBA_HEAD_DIM)
    kv = jnp.pad(kv, ((0, 0), (0, t_pad - T), (0, 0), (0, 0), (0, 0)))
    kv = kv.transpose(2, 0, 3, 1, 4).reshape(2, B, MOBA_HEADS, nb, MOBA_BLOCK, MOBA_HEAD_DIM)
    kb, vb = kv[0], kv[1]
    kmean = jnp.mean(kb.astype(jnp.float32), axis=3).astype(kb.dtype)
    return kb, vb, kmean


def moba_attention(q, kb, vb, kmean):
    B, T, H, dh = q.shape
    nb = kb.shape[2]
    t_pad = nb * MOBA_BLOCK
    q = jnp.pad(q, ((0, 0), (0, t_pad - T), (0, 0), (0, 0))).transpose(0, 2, 1, 3) * (dh ** -0.5)
    q_blk = jnp.arange(t_pad) // MOBA_BLOCK
    gate = jnp.einsum('bhtd,bhnd->bhtn', q, kmean).astype(jnp.float32)
    gate = jnp.where(jnp.arange(nb)[None, :] < q_blk[:, None], gate, -jnp.inf)
    n_sel = min(MOBA_TOP_K, nb)
    _, sel = lax.top_k(gate, n_sel)
    n_chunks = t_pad // MOBA_Q_CHUNK
    q_c = q.reshape(B, H, n_chunks, MOBA_Q_CHUNK, dh).transpose(2, 0, 1, 3, 4)
    sel_c = sel.reshape(B, H, n_chunks, MOBA_Q_CHUNK, n_sel).transpose(2, 0, 1, 3, 4)
    bi = jnp.arange(B)[:, None, None, None]
    hi = jnp.arange(H)[None, :, None, None]

    def one_chunk(args):
        c, qc, sc = args
        q_pos = c * MOBA_Q_CHUNK + jnp.arange(MOBA_Q_CHUNK)
        own = (c * MOBA_Q_CHUNK) // MOBA_BLOCK
        k_sel = kb[bi, hi, sc]
        v_sel = vb[bi, hi, sc]
        s_sel = jnp.einsum('bhqd,bhqskd->bhqsk', qc, k_sel).astype(jnp.float32)
        valid = jnp.arange(n_sel)[None, :] < (q_pos // MOBA_BLOCK)[:, None]
        s_sel = jnp.where(valid[None, None, :, :, None], s_sel, -jnp.inf)
        s_sel = s_sel.reshape(B, H, MOBA_Q_CHUNK, n_sel * MOBA_BLOCK)
        k_own = lax.dynamic_index_in_dim(kb, own, axis=2, keepdims=False)
        v_own = lax.dynamic_index_in_dim(vb, own, axis=2, keepdims=False)
        s_own = jnp.einsum('bhqd,bhkd->bhqk', qc, k_own).astype(jnp.float32)
        k_pos = own * MOBA_BLOCK + jnp.arange(MOBA_BLOCK)
        s_own = jnp.where(k_pos[None, :] <= q_pos[:, None], s_own, -jnp.inf)
        p = jax.nn.softmax(jnp.concatenate([s_sel, s_own], axis=-1), axis=-1)
        p_sel = p[..., :n_sel * MOBA_BLOCK].reshape(B, H, MOBA_Q_CHUNK, n_sel, MOBA_BLOCK).astype(vb.dtype)
        p_own = p[..., n_sel * MOBA_BLOCK:].astype(vb.dtype)
        return (jnp.einsum('bhqsk,bhqskd->bhqd', p_sel, v_sel)
                + jnp.einsum('bhqk,bhkd->bhqd', p_own, v_own))

    o = lax.map(one_chunk, (jnp.arange(n_chunks), q_c, sel_c))
    o = o.transpose(1, 0, 3, 2, 4).reshape(B, t_pad, H, dh)
    return o[:, :T]


def moba_mixer(h, w_in, w_out, kb, vb, kmean):
    B, T, _ = h.shape
    q, z = jnp.split(h @ w_in, 2, axis=-1)
    o = moba_attention(q.reshape(B, T, MOBA_HEADS, MOBA_HEAD_DIM), kb, vb, kmean)
    o = o.reshape(B, T, MOBA_WIDTH) * jax.nn.silu(z)
    return o @ w_out


def setup_inputs(seed: int = 0) -> dict:
    key = jax.random.key(seed)
    ks = jax.random.split(key, 16)
    nrm = jax.random.normal
    f32 = jnp.float32
    x = nrm(ks[0], (BATCH, SEQ, D_MODEL), f32)
    a_w_in = nrm(ks[1], (N_A_LAYERS, D_MODEL, A_IN_COLS), f32) * D_MODEL ** -0.5
    a_conv_w = nrm(ks[2], (N_A_LAYERS, CONV_K, 3 * GDN_WIDTH), f32) * CONV_K ** -0.5
    a_A_log = jnp.log(jax.random.uniform(ks[3], (N_A_LAYERS, GDN_HEADS), f32, 1.0, 16.0))
    dt = jnp.exp(jax.random.uniform(ks[4], (N_A_LAYERS, GDN_HEADS), f32,
                                    math.log(1e-3), math.log(1e-1)))
    a_dt_bias = jnp.log(jnp.expm1(dt))
    a_norm_w = 1.0 + 0.02 * nrm(ks[5], (N_A_LAYERS, GDN_HEAD_DIM), f32)
    a_w_out = nrm(ks[6], (N_A_LAYERS, GDN_WIDTH, D_MODEL), f32) * GDN_WIDTH ** -0.5 * DEEPNORM_BETA
    a_ln_g = 1.0 + 0.02 * nrm(ks[7], (N_A_LAYERS, D_MODEL), f32)
    a_ln_b = 0.02 * nrm(ks[8], (N_A_LAYERS, D_MODEL), f32)
    b_w_kv = nrm(ks[9], (D_MODEL, 2 * MOBA_WIDTH), f32) * D_MODEL ** -0.5
    b_w_in = nrm(ks[10], (N_B_LAYERS, D_MODEL, 2 * MOBA_WIDTH), f32) * D_MODEL ** -0.5
    b_w_out = nrm(ks[11], (N_B_LAYERS, MOBA_WIDTH, D_MODEL), f32) * MOBA_WIDTH ** -0.5 * DEEPNORM_BETA
    b_ln_g = 1.0 + 0.02 * nrm(ks[12], (N_B_LAYERS, D_MODEL), f32)
    b_ln_b = 0.02 * nrm(ks[13], (N_B_LAYERS, D_MODEL), f32)
    return {"x": x, "a_w_in": a_w_in, "a_conv_w": a_conv_w, "a_A_log": a_A_log,
            "a_dt_bias": a_dt_bias, "a_norm_w": a_norm_w, "a_w_out": a_w_out,
            "a_ln_g": a_ln_g, "a_ln_b": a_ln_b, "b_w_kv": b_w_kv, "b_w_in": b_w_in,
            "b_w_out": b_w_out, "b_ln_g": b_ln_g, "b_ln_b": b_ln_b}


def reference(x, a_w_in, a_conv_w, a_A_log, a_dt_bias, a_norm_w, a_w_out, a_ln_g, a_ln_b,
              b_w_kv, b_w_in, b_w_out, b_ln_g, b_ln_b):
    kb = vb = kmean = None
    for layer in range(DEPTH):
        if layer < N_A_LAYERS:
            i = layer
            y = gated_deltanet_mixer(x, a_w_in[i], a_conv_w[i], a_A_log[i], a_dt_bias[i],
                                     a_norm_w[i], a_w_out[i])
            x = layer_norm(DEEPNORM_ALPHA * x + y, a_ln_g[i], a_ln_b[i])
            if layer == N_A_LAYERS - 1:
                kb, vb, kmean = shared_kv(x, b_w_kv)
        else:
            i = layer - N_A_LAYERS
            y = moba_mixer(x, b_w_in[i], b_w_out[i], kb, vb, kmean)
            x = layer_norm(DEEPNORM_ALPHA * x + y, b_ln_g[i], b_ln_b[i])
    return x
```

```python
import functools

import jax
import jax.numpy as jnp
from jax import lax
from jax.experimental import pallas as pl
from jax.experimental.pallas import tpu as pltpu

F32 = jnp.float32
BF16 = jnp.bfloat16

D_MODEL = 1024
HEADS = 8
HEAD_DIM = 128
CONV_K = 4
GDN_CHUNK = 64
GDN_BLOCK = 256
MOBA_BLOCK = 256
MOBA_TOP_K = 3
DEPTH = 4
DEEPNORM_ALPHA = (2 * DEPTH) ** 0.25
LN_EPS = 1e-5
RMS_EPS = 1e-6
MASK_VALUE = -1e30
GATE_LANES = 128
V7X_VMEM_LIMIT = 56 * 1024 * 1024

_NT = (((1,), (1,)), ((), ()))


def _params(*sem):
    return pltpu.CompilerParams(dimension_semantics=sem, vmem_limit_bytes=V7X_VMEM_LIMIT)


def _silu(x):
    return x * jax.nn.sigmoid(x)


def _mm_kernel(x_ref, w_ref, o_ref):
    o_ref[...] = jnp.dot(x_ref[...].astype(BF16), w_ref[...],
                         preferred_element_type=F32).astype(o_ref.dtype)


def _matmul(x, w, out_dtype, tm, tn, name):
    m, k = x.shape
    n = w.shape[1]
    return pl.pallas_call(
        _mm_kernel,
        grid=(m // tm, n // tn),
        in_specs=[pl.BlockSpec((tm, k), lambda i, j: (i, 0)),
                  pl.BlockSpec((k, tn), lambda i, j: (0, j))],
        out_specs=pl.BlockSpec((tm, tn), lambda i, j: (i, j)),
        out_shape=jax.ShapeDtypeStruct((m, n), out_dtype),
        compiler_params=_params("parallel", "arbitrary"),
        name=name,
    )(x, w)


def _out_ln_kernel(o_ref, w_ref, x_ref, g_ref, b_ref, out_ref, y_ref):
    y_ref[...] = jnp.dot(o_ref[...], w_ref[...], preferred_element_type=F32)
    rows = 64
    for r in range(o_ref.shape[0] // rows):
        rs = slice(r * rows, (r + 1) * rows)
        s = DEEPNORM_ALPHA * x_ref[rs, :] + y_ref[rs, :]
        mu = jnp.mean(s, axis=-1, keepdims=True)
        c = s - mu
        var = jnp.mean(c * c, axis=-1, keepdims=True)
        out_ref[rs, :] = c * lax.rsqrt(var + LN_EPS) * g_ref[...] + b_ref[...]


def _out_proj_ln(o, w, x, g, b, tm, name):
    m, k = o.shape
    n = w.shape[1]
    return pl.pallas_call(
        _out_ln_kernel,
        grid=(m // tm,),
        in_specs=[pl.BlockSpec((tm, k), lambda i: (i, 0)),
                  pl.BlockSpec((k, n), lambda i: (0, 0)),
                  pl.BlockSpec((tm, n), lambda i: (i, 0)),
                  pl.BlockSpec((1, n), lambda i: (0, 0)),
                  pl.BlockSpec((1, n), lambda i: (0, 0))],
        out_specs=pl.BlockSpec((tm, n), lambda i: (i, 0)),
        out_shape=jax.ShapeDtypeStruct((m, n), F32),
        scratch_shapes=[pltpu.VMEM((tm, n), F32)],
        compiler_params=_params("parallel"),
        name=name,
    )(o, w, x, g, b)


def _gdn_prep_kernel(q_ref, k_ref, v_ref, qp_ref, kp_ref, vp_ref, gates_ref, cw_ref, alog_ref, dtb_ref,
                     wq_ref, u_ref, kdt_ref, attn_ref, egl_ref, xs_ref):
    tb = GDN_BLOCK
    nck = tb // GDN_CHUNK
    first = pl.program_id(1) == 0

    gts = gates_ref[...]
    beta_all = jax.nn.sigmoid(gts)
    xg = gts + dtb_ref[...]
    softplus = jnp.maximum(xg, 0.0) + jnp.log1p(jnp.exp(-jnp.abs(xg)))
    g_all = -jnp.exp(alog_ref[...]) * softplus
    row = lax.broadcasted_iota(jnp.int32, (tb, GATE_LANES), 0)
    rin = jnp.bitwise_and(row, GDN_CHUNK - 1)
    gc = g_all
    s = 1
    while s < GDN_CHUNK:
        gc = gc + jnp.where(rin >= s, pltpu.roll(gc, s, axis=0), 0.0)
        s *= 2
    gc3 = gc.reshape(nck, GDN_CHUNK, GATE_LANES)
    gl = jnp.broadcast_to(gc3[:, GDN_CHUNK - 1:GDN_CHUNK, :], gc3.shape).reshape(tb, GATE_LANES)
    egc = jnp.exp(gc)
    ekd = jnp.exp(gl - gc)
    egl_ref[...] = jnp.exp(gl)
    gct = gc.T

    ri = lax.broadcasted_iota(jnp.int32, (tb, tb), 0)
    ci = lax.broadcasted_iota(jnp.int32, (tb, tb), 1)
    same = jnp.right_shift(ri, 6) == jnp.right_shift(ci, 6)
    causal = same & (ri >= ci)
    strict = same & (ri > ci)
    eye = (ri == ci).astype(F32)

    def conv_silu(x_ref, p_ref, slot, h, col0):
        sl = slice(h * HEAD_DIM, (h + 1) * HEAD_DIM)
        xs_ref[slot, 0:8, :] = jnp.where(first, 0.0, p_ref[:, sl])
        xs_ref[slot, 8:8 + tb, :] = x_ref[:, sl]
        cw = cw_ref[:, col0 + h * HEAD_DIM:col0 + (h + 1) * HEAD_DIM]
        acc = xs_ref[slot, 8:8 + tb, :] * cw[CONV_K - 1:CONV_K, :]
        for d in range(1, CONV_K):
            acc = acc + xs_ref[slot, 8 - d:8 - d + tb, :] * cw[CONV_K - 1 - d:CONV_K - d, :]
        return _silu(acc)

    for h in range(HEADS):
        sl = slice(h * HEAD_DIM, (h + 1) * HEAD_DIM)
        qc = conv_silu(q_ref, qp_ref, 3 * h, h, 0)
        kc = conv_silu(k_ref, kp_ref, 3 * h + 1, h, HEADS * HEAD_DIM)
        vc = conv_silu(v_ref, vp_ref, 3 * h + 2, h, 2 * HEADS * HEAD_DIM)
        qn = qc * lax.rsqrt(jnp.sum(qc * qc, axis=-1, keepdims=True) + RMS_EPS)
        kn = kc * lax.rsqrt(jnp.sum(kc * kc, axis=-1, keepdims=True) + RMS_EPS)
        beta = beta_all[:, h:h + 1]
        gcol = gc[:, HEADS + h:HEADS + h + 1]
        egc_c = egc[:, HEADS + h:HEADS + h + 1]
        ekd_c = ekd[:, HEADS + h:HEADS + h + 1]
        grow = gct[HEADS + h:HEADS + h + 1, :]

        qs = qn * (HEAD_DIM ** -0.5)
        kb = kn * beta
        kn_b = kn.astype(BF16)
        kk = lax.dot_general(kb.astype(BF16), kn_b, _NT, preferred_element_type=F32)
        qk = lax.dot_general(qs.astype(BF16), kn_b, _NT, preferred_element_type=F32)
        dm = jnp.exp(jnp.where(causal, gcol - grow, -jnp.inf))
        attn_ref[:, h * tb:(h + 1) * tb] = (qk * dm).astype(BF16)

        nm = jnp.where(strict, -(kk * dm), 0.0)
        tm = eye + nm
        p = nm.astype(BF16)
        steps = GDN_CHUNK.bit_length() - 2
        for _ in range(steps):
            p = jnp.dot(p, p, preferred_element_type=F32).astype(BF16)
            tm = tm + jnp.dot(tm.astype(BF16), p, preferred_element_type=F32)
        rhs = jnp.concatenate([(kb * egc_c).astype(BF16), (vc * beta).astype(BF16)], axis=1)
        wu = jnp.dot(tm.astype(BF16), rhs, preferred_element_type=F32)
        w = wu[:, :HEAD_DIM].astype(BF16)
        qd = (qs * egc_c).astype(BF16)
        u_ref[:, sl] = wu[:, HEAD_DIM:]
        for c in range(nck):
            rs = slice(c * GDN_CHUNK, (c + 1) * GDN_CHUNK)
            wq_ref[2 * c * GDN_CHUNK:(2 * c + 1) * GDN_CHUNK, sl] = w[rs, :]
            wq_ref[(2 * c + 1) * GDN_CHUNK:(2 * c + 2) * GDN_CHUNK, sl] = qd[rs, :]
        kdt_ref[0, sl, :] = (kn * ekd_c).T.astype(BF16)


def _gdn_prep(proj, gates, conv_w, alog_row, dtb_row, batch, seq):
    tb = GDN_BLOCK
    nt = seq // tb
    bt = batch * seq
    width = HEADS * HEAD_DIM
    blk = lambda col: pl.BlockSpec((tb, width), lambda b, t: (b * nt + t, col))
    prev = lambda col: pl.BlockSpec((8, width), lambda b, t: (jnp.maximum((b * nt + t) * (tb // 8) - 1, 0), col))
    return pl.pallas_call(
        _gdn_prep_kernel,
        grid=(batch, nt),
        in_specs=[blk(0), blk(1), blk(2), prev(0), prev(1), prev(2),
                  pl.BlockSpec((tb, GATE_LANES), lambda b, t: (b * nt + t, 0)),
                  pl.BlockSpec((CONV_K, 3 * width), lambda b, t: (0, 0)),
                  pl.BlockSpec((1, GATE_LANES), lambda b, t: (0, 0)),
                  pl.BlockSpec((1, GATE_LANES), lambda b, t: (0, 0))],
        out_specs=[pl.BlockSpec((2 * tb, width), lambda b, t: (b * nt + t, 0)),
                   pl.BlockSpec((tb, width), lambda b, t: (b * nt + t, 0)),
                   pl.BlockSpec((1, width, tb), lambda b, t: (b * nt + t, 0, 0)),
                   pl.BlockSpec((tb, HEADS * tb), lambda b, t: (b * nt + t, 0)),
                   pl.BlockSpec((tb, GATE_LANES), lambda b, t: (b * nt + t, 0))],
        out_shape=[jax.ShapeDtypeStruct((2 * bt, width), BF16),
                   jax.ShapeDtypeStruct((bt, width), F32),
                   jax.ShapeDtypeStruct((bt // tb, width, tb), BF16),
                   jax.ShapeDtypeStruct((bt, HEADS * tb), BF16),
                   jax.ShapeDtypeStruct((bt, GATE_LANES), F32)],
        scratch_shapes=[pltpu.VMEM((3 * HEADS, tb + 8, HEAD_DIM), F32)],
        compiler_params=_params("parallel", "arbitrary"),
        name="gdn_prep",
    )(proj, proj, proj, proj, proj, proj, gates, conv_w, alog_row, dtb_row)


def _gdn_scan_kernel(wq_ref, u_ref, kdt_ref, attn_ref, egl_ref, z_ref, nw_ref, o_ref, s_ref, vn_ref):
    tb = GDN_BLOCK
    nck = tb // GDN_CHUNK
    ck = GDN_CHUNK

    @pl.when(pl.program_id(1) == 0)
    def _():
        s_ref[...] = jnp.zeros_like(s_ref)

    zeros_half = jnp.zeros((ck, HEAD_DIM), BF16)
    for h in range(HEADS):
        sl = slice(h * HEAD_DIM, (h + 1) * HEAD_DIM)
        state = s_ref[h]
        inter = []
        for c in range(nck):
            rs = slice(c * ck, (c + 1) * ck)
            r = jnp.dot(wq_ref[2 * c * ck:(2 * c + 2) * ck, sl], state.astype(BF16),
                        preferred_element_type=F32)
            vnew = (u_ref[rs, sl] - r[:ck, :]).astype(BF16)
            inter.append(r[ck:, :])
            vn_ref[h, rs, :] = vnew
            vpad = jnp.concatenate([vnew, zeros_half] if c % 2 == 0 else [zeros_half, vnew], axis=0)
            slab = kdt_ref[0, sl, (c // 2) * 2 * ck:(c // 2 + 1) * 2 * ck]
            cd = egl_ref[c * ck:c * ck + 1, HEADS + h:HEADS + h + 1]
            state = state * cd + jnp.dot(slab, vpad, preferred_element_type=F32)
        s_ref[h] = state
        o = jnp.concatenate(inter, axis=0) + jnp.dot(attn_ref[:, h * tb:(h + 1) * tb], vn_ref[h],
                                                     preferred_element_type=F32)
        o = o * lax.rsqrt(jnp.mean(o * o, axis=-1, keepdims=True) + RMS_EPS) * nw_ref[...]
        o_ref[:, sl] = (o * _silu(z_ref[:, sl])).astype(BF16)


def _gdn_scan(wq, u, kdt, attn, egl, proj, norm_w, batch, seq):
    tb = GDN_BLOCK
    nt = seq // tb
    bt = batch * seq
    width = HEADS * HEAD_DIM
    return pl.pallas_call(
        _gdn_scan_kernel,
        grid=(batch, nt),
        in_specs=[pl.BlockSpec((2 * tb, width), lambda b, t: (b * nt + t, 0)),
                  pl.BlockSpec((tb, width), lambda b, t: (b * nt + t, 0)),
                  pl.BlockSpec((1, width, tb), lambda b, t: (b * nt + t, 0, 0)),
                  pl.BlockSpec((tb, HEADS * tb), lambda b, t: (b * nt + t, 0)),
                  pl.BlockSpec((tb, GATE_LANES), lambda b, t: (b * nt + t, 0)),
                  pl.BlockSpec((tb, width), lambda b, t: (b * nt + t, 3)),
                  pl.BlockSpec((1, HEAD_DIM), lambda b, t: (0, 0))],
        out_specs=pl.BlockSpec((tb, width), lambda b, t: (b * nt + t, 0)),
        out_shape=jax.ShapeDtypeStruct((bt, width), BF16),
        scratch_shapes=[pltpu.VMEM((HEADS, HEAD_DIM, HEAD_DIM), F32),
                        pltpu.VMEM((HEADS, tb, HEAD_DIM), BF16)],
        compiler_params=_params("parallel", "arbitrary"),
        name="gdn_scan",
    )(wq, u, kdt, attn, egl, proj, norm_w)


def _gdn_layer(x, w_in, conv_w, a_log, dt_bias, norm_w, w_out, ln_g, ln_b, batch, seq):
    width = HEADS * HEAD_DIM
    w_main = w_in[:, :4 * width].astype(BF16)
    w_gate = jnp.pad(w_in[:, 4 * width:], ((0, 0), (0, GATE_LANES - 2 * HEADS))).astype(BF16)
    pad_row = lambda v: jnp.pad(v.astype(F32), (HEADS, GATE_LANES - 2 * HEADS))[None, :]
    proj = _matmul(x, w_main, F32, 512, 1024, "gdn_in_proj")
    gates = _matmul(x, w_gate, F32, 1024, GATE_LANES, "gdn_gate_proj")
    wq, u, kdt, attn, egl = _gdn_prep(proj, gates, conv_w.astype(F32), pad_row(a_log), pad_row(dt_bias), batch, seq)
    o = _gdn_scan(wq, u, kdt, attn, egl, proj, norm_w.astype(F32)[None, :], batch, seq)
    return _out_proj_ln(o, w_out.astype(BF16), x, ln_g[None, :], ln_b[None, :], 512, "gdn_out_ln")


def _kv_kernel(x_ref, w_ref, kv_ref, kmean_ref):
    blk = MOBA_BLOCK
    width = HEADS * HEAD_DIM
    means = []
    for r in range(x_ref.shape[0] // blk):
        rs = slice(r * blk, (r + 1) * blk)
        acc = jnp.dot(x_ref[rs, :].astype(BF16), w_ref[...], preferred_element_type=F32)
        kv_ref[rs, :] = acc.astype(BF16)
        means.append(jnp.mean(acc[:, :width], axis=0, keepdims=True))
    kmean_ref[...] = jnp.concatenate(means, axis=0)


def _shared_kv(x, w_kv):
    m, k = x.shape
    n = w_kv.shape[1]
    tm = 8 * MOBA_BLOCK
    return pl.pallas_call(
        _kv_kernel,
        grid=(m // tm,),
        in_specs=[pl.BlockSpec((tm, k), lambda i: (i, 0)),
                  pl.BlockSpec((k, n), lambda i: (0, 0))],
        out_specs=[pl.BlockSpec((tm, n), lambda i: (i, 0)),
                   pl.BlockSpec((tm // MOBA_BLOCK, n // 2), lambda i: (i, 0))],
        out_shape=[jax.ShapeDtypeStruct((m, n), BF16),
                   jax.ShapeDtypeStruct((m // MOBA_BLOCK, n // 2), F32)],
        compiler_params=_params("parallel"),
        name="moba_shared_kv",
    )(x, w_kv)


def _moba_attn_kernel(q_ref, z_ref, k_ref, v_ref, kmean_ref, o_ref, kaug_ref, vaug_ref, acc_ref):
    blk = MOBA_BLOCK
    seq = k_ref.shape[0]
    nb = seq // blk
    qb = pl.program_id(2)

    @pl.when(qb == 0)
    def _():
        kaug_ref[:, :HEAD_DIM] = k_ref[...]
        vaug_ref[:, :HEAD_DIM] = v_ref[...]
        trow = lax.broadcasted_iota(jnp.int32, (seq, HEAD_DIM), 0)
        lane = lax.broadcasted_iota(jnp.int32, (seq, HEAD_DIM), 1)
        kaug_ref[:, HEAD_DIM:] = (jnp.right_shift(trow, 8) == lane).astype(BF16)
        vaug_ref[:, HEAD_DIM:] = (lane == 0).astype(BF16)

    q = q_ref[...] * (HEAD_DIM ** -0.5)
    q_b = q.astype(BF16)
    kmean = jnp.concatenate([kmean_ref[...], jnp.zeros((HEAD_DIM - nb, HEAD_DIM), F32)], axis=0)
    gate = lax.dot_general(q_b, kmean.astype(BF16), _NT, preferred_element_type=F32)
    lane = lax.broadcasted_iota(jnp.int32, (blk, HEAD_DIM), 1)
    valid = lane < qb
    g = jnp.where(valid, gate, -jnp.inf)
    rank = jnp.zeros((blk, HEAD_DIM), jnp.int32)
    for i in range(nb):
        gi = g[:, i:i + 1]
        rank = rank + ((gi > g) | ((gi == g) & (i < lane))).astype(jnp.int32)
    keep = (valid & (rank < MOBA_TOP_K)) | (lane == qb)
    bias = jnp.where(keep, 0.0, MASK_VALUE)
    q_aug = jnp.concatenate([q_b, bias.astype(BF16)], axis=1)

    own = pl.multiple_of(qb * blk, blk)
    s = lax.dot_general(q_aug, kaug_ref[pl.ds(own, blk), :], _NT, preferred_element_type=F32)
    qi = lax.broadcasted_iota(jnp.int32, (blk, blk), 0)
    ki = lax.broadcasted_iota(jnp.int32, (blk, blk), 1)
    s = jnp.where(ki <= qi, s, MASK_VALUE)
    m0 = jnp.max(s, axis=-1, keepdims=True)
    p = jnp.exp(s - m0)
    acc_ref[...] = jnp.dot(p.astype(BF16), vaug_ref[pl.ds(own, blk), :], preferred_element_type=F32)

    def body(j, m):
        start = pl.multiple_of(j * blk, blk)
        sj = lax.dot_general(q_aug, kaug_ref[pl.ds(start, blk), :], _NT, preferred_element_type=F32)
        m_new = jnp.maximum(m, jnp.max(sj, axis=-1, keepdims=True))
        pj = jnp.exp(sj - m_new)
        acc_ref[...] = jnp.exp(m - m_new) * acc_ref[...] + jnp.dot(
            pj.astype(BF16), vaug_ref[pl.ds(start, blk), :], preferred_element_type=F32)
        return m_new

    lax.fori_loop(0, qb, body, m0)
    acc = acc_ref[...]
    o = acc[:, :HEAD_DIM] / acc[:, HEAD_DIM:HEAD_DIM + 1]
    o_ref[...] = (o * _silu(z_ref[...])).astype(BF16)


def _moba_attn(qz, kv, kmean, batch, seq):
    blk = MOBA_BLOCK
    nq = seq // blk
    bt = batch * seq
    width = HEADS * HEAD_DIM
    return pl.pallas_call(
        _moba_attn_kernel,
        grid=(batch, HEADS, nq),
        in_specs=[pl.BlockSpec((blk, HEAD_DIM), lambda b, h, i: (b * nq + i, h)),
                  pl.BlockSpec((blk, HEAD_DIM), lambda b, h, i: (b * nq + i, HEADS + h)),
                  pl.BlockSpec((seq, HEAD_DIM), lambda b, h, i: (b, h)),
                  pl.BlockSpec((seq, HEAD_DIM), lambda b, h, i: (b, HEADS + h)),
                  pl.BlockSpec((nq, HEAD_DIM), lambda b, h, i: (b, h))],
        out_specs=pl.BlockSpec((blk, HEAD_DIM), lambda b, h, i: (b * nq + i, h)),
        out_shape=jax.ShapeDtypeStruct((bt, width), BF16),
        scratch_shapes=[pltpu.VMEM((seq, 2 * HEAD_DIM), BF16),
                        pltpu.VMEM((seq, 2 * HEAD_DIM), BF16),
                        pltpu.VMEM((blk, 2 * HEAD_DIM), F32)],
        compiler_params=_params("parallel", "parallel", "arbitrary"),
        name="moba_attn",
    )(qz, qz, kv, kv, kmean)


def _moba_layer(x, w_in, w_out, ln_g, ln_b, kv, kmean, batch, seq):
    qz = _matmul(x, w_in.astype(BF16), F32, 512, 1024, "moba_in_proj")
    o = _moba_attn(qz, kv, kmean, batch, seq)
    return _out_proj_ln(o, w_out.astype(BF16), x, ln_g[None, :], ln_b[None, :], 512, "moba_out_ln")


def kernel(x, a_w_in, a_conv_w, a_A_log, a_dt_bias, a_norm_w, a_w_out, a_ln_g, a_ln_b, b_w_kv, b_w_in, b_w_out, b_ln_g, b_ln_b):
    batch, seq, d_model = x.shape
    assert d_model == D_MODEL and seq % (8 * MOBA_BLOCK) == 0 and seq // MOBA_BLOCK <= HEAD_DIM
    assert a_w_in.shape[2] == 4 * HEADS * HEAD_DIM + 2 * HEADS and a_conv_w.shape[1] == CONV_K
    h = x.reshape(batch * seq, d_model)
    for i in range(a_w_in.shape[0]):
        h = _gdn_layer(h, a_w_in[i], a_conv_w[i], a_A_log[i], a_dt_bias[i], a_norm_w[i], a_w_out[i],
                       a_ln_g[i], a_ln_b[i], batch, seq)
    kv, kmean = _shared_kv(h, b_w_kv.astype(BF16))
    for i in range(b_w_in.shape[0]):
        h = _moba_layer(h, b_w_in[i], b_w_out[i], b_ln_g[i], b_ln_b[i], kv, kmean, batch, seq)
    return h.reshape(batch, seq, d_model)
```

```python
import functools

import jax
import jax.numpy as jnp
from jax import lax
from jax.experimental import pallas as pl
from jax.experimental.pallas import tpu as pltpu

F32 = jnp.float32
BF16 = jnp.bfloat16

D_MODEL = 1024
HEADS = 8
HEAD_DIM = 128
CONV_K = 4
GDN_CHUNK = 64
GDN_BLOCK = 256
MOBA_BLOCK = 256
MOBA_TOP_K = 3
MOBA_GROUP = 4
MOBA_HEADS_PER_STEP = 2
DEPTH = 4
DEEPNORM_ALPHA = (2 * DEPTH) ** 0.25
LN_EPS = 1e-5
RMS_EPS = 1e-6
MASK_VALUE = -1e30
GATE_LANES = 128
V7X_VMEM_LIMIT = 56 * 1024 * 1024

_NT = (((1,), (1,)), ((), ()))


def _params(*sem):
    return pltpu.CompilerParams(dimension_semantics=sem, vmem_limit_bytes=V7X_VMEM_LIMIT)


def _silu(x):
    return x * jax.nn.sigmoid(x)


def _mm_kernel(x_ref, w_ref, o_ref):
    o_ref[...] = jnp.dot(x_ref[...].astype(BF16), w_ref[...],
                         preferred_element_type=F32).astype(o_ref.dtype)


def _matmul(x, w, out_dtype, tm, tn, name):
    m, k = x.shape
    n = w.shape[1]
    return pl.pallas_call(
        _mm_kernel,
        grid=(m // tm, n // tn),
        in_specs=[pl.BlockSpec((tm, k), lambda i, j: (i, 0)),
                  pl.BlockSpec((k, tn), lambda i, j: (0, j))],
        out_specs=pl.BlockSpec((tm, tn), lambda i, j: (i, j)),
        out_shape=jax.ShapeDtypeStruct((m, n), out_dtype),
        compiler_params=_params("parallel", "arbitrary"),
        name=name,
    )(x, w)


def _out_ln_kernel(o_ref, w_ref, x_ref, g_ref, b_ref, out_ref, y_ref):
    y_ref[...] = jnp.dot(o_ref[...], w_ref[...], preferred_element_type=F32)
    rows = 64
    for r in range(o_ref.shape[0] // rows):
        rs = slice(r * rows, (r + 1) * rows)
        s = DEEPNORM_ALPHA * x_ref[rs, :] + y_ref[rs, :]
        mu = jnp.mean(s, axis=-1, keepdims=True)
        c = s - mu
        var = jnp.mean(c * c, axis=-1, keepdims=True)
        out_ref[rs, :] = c * lax.rsqrt(var + LN_EPS) * g_ref[...] + b_ref[...]


def _out_proj_ln(o, w, x, g, b, tm, name):
    m, k = o.shape
    n = w.shape[1]
    return pl.pallas_call(
        _out_ln_kernel,
        grid=(m // tm,),
        in_specs=[pl.BlockSpec((tm, k), lambda i: (i, 0)),
                  pl.BlockSpec((k, n), lambda i: (0, 0)),
                  pl.BlockSpec((tm, n), lambda i: (i, 0)),
                  pl.BlockSpec((1, n), lambda i: (0, 0)),
                  pl.BlockSpec((1, n), lambda i: (0, 0))],
        out_specs=pl.BlockSpec((tm, n), lambda i: (i, 0)),
        out_shape=jax.ShapeDtypeStruct((m, n), F32),
        scratch_shapes=[pltpu.VMEM((tm, n), F32)],
        compiler_params=_params("parallel"),
        name=name,
    )(o, w, x, g, b)


def _gdn_prep_kernel(q_ref, k_ref, v_ref, qp_ref, kp_ref, vp_ref, gates_ref, cw_ref, alog_ref, dtb_ref,
                     wq_ref, u_ref, kdt_ref, attn_ref, egl_ref, xs_ref):
    tb = GDN_BLOCK
    nck = tb // GDN_CHUNK
    first = pl.program_id(1) == 0

    gts = gates_ref[...]
    beta_all = jax.nn.sigmoid(gts)
    xg = gts + dtb_ref[...]
    softplus = jnp.maximum(xg, 0.0) + jnp.log1p(jnp.exp(-jnp.abs(xg)))
    g_all = -jnp.exp(alog_ref[...]) * softplus
    row = lax.broadcasted_iota(jnp.int32, (tb, GATE_LANES), 0)
    rin = jnp.bitwise_and(row, GDN_CHUNK - 1)
    gc = g_all
    s = 1
    while s < GDN_CHUNK:
        gc = gc + jnp.where(rin >= s, pltpu.roll(gc, s, axis=0), 0.0)
        s *= 2
    gc3 = gc.reshape(nck, GDN_CHUNK, GATE_LANES)
    gl = jnp.broadcast_to(gc3[:, GDN_CHUNK - 1:GDN_CHUNK, :], gc3.shape).reshape(tb, GATE_LANES)
    egc = jnp.exp(gc)
    ekd = jnp.exp(gl - gc)
    egl_ref[...] = jnp.exp(gl)
    gct = gc.T

    ri = lax.broadcasted_iota(jnp.int32, (tb, tb), 0)
    ci = lax.broadcasted_iota(jnp.int32, (tb, tb), 1)
    same = jnp.right_shift(ri, 6) == jnp.right_shift(ci, 6)
    causal = same & (ri >= ci)
    strict = same & (ri > ci)
    eye = (ri == ci).astype(F32)

    def conv_silu(x_ref, p_ref, slot, h, col0):
        sl = slice(h * HEAD_DIM, (h + 1) * HEAD_DIM)
        xs_ref[slot, 0:8, :] = jnp.where(first, 0.0, p_ref[:, sl])
        xs_ref[slot, 8:8 + tb, :] = x_ref[:, sl]
        cw = cw_ref[:, col0 + h * HEAD_DIM:col0 + (h + 1) * HEAD_DIM]
        acc = xs_ref[slot, 8:8 + tb, :] * cw[CONV_K - 1:CONV_K, :]
        for d in range(1, CONV_K):
            acc = acc + xs_ref[slot, 8 - d:8 - d + tb, :] * cw[CONV_K - 1 - d:CONV_K - d, :]
        return _silu(acc)

    for h in range(HEADS):
        sl = slice(h * HEAD_DIM, (h + 1) * HEAD_DIM)
        qc = conv_silu(q_ref, qp_ref, 3 * h, h, 0)
        kc = conv_silu(k_ref, kp_ref, 3 * h + 1, h, HEADS * HEAD_DIM)
        vc = conv_silu(v_ref, vp_ref, 3 * h + 2, h, 2 * HEADS * HEAD_DIM)
        qn = qc * lax.rsqrt(jnp.sum(qc * qc, axis=-1, keepdims=True) + RMS_EPS)
        kn = kc * lax.rsqrt(jnp.sum(kc * kc, axis=-1, keepdims=True) + RMS_EPS)
        beta = beta_all[:, h:h + 1]
        gcol = gc[:, HEADS + h:HEADS + h + 1]
        egc_c = egc[:, HEADS + h:HEADS + h + 1]
        ekd_c = ekd[:, HEADS + h:HEADS + h + 1]
        grow = gct[HEADS + h:HEADS + h + 1, :]

        qs = qn * (HEAD_DIM ** -0.5)
        kb = kn * beta
        kn_b = kn.astype(BF16)
        kk = lax.dot_general(kb.astype(BF16), kn_b, _NT, preferred_element_type=F32)
        qk = lax.dot_general(qs.astype(BF16), kn_b, _NT, preferred_element_type=F32)
        dm = jnp.exp(jnp.where(causal, gcol - grow, -jnp.inf))
        attn_ref[:, h * tb:(h + 1) * tb] = (qk * dm).astype(BF16)

        nm = jnp.where(strict, -(kk * dm), 0.0)
        tm = eye + nm
        p = nm.astype(BF16)
        steps = GDN_CHUNK.bit_length() - 2
        for _ in range(steps):
            p = jnp.dot(p, p, preferred_element_type=F32).astype(BF16)
            tm = tm + jnp.dot(tm.astype(BF16), p, preferred_element_type=F32)
        rhs = jnp.concatenate([(kb * egc_c).astype(BF16), (vc * beta).astype(BF16)], axis=1)
        wu = jnp.dot(tm.astype(BF16), rhs, preferred_element_type=F32)
        w = wu[:, :HEAD_DIM].astype(BF16)
        qd = (qs * egc_c).astype(BF16)
        u_ref[:, sl] = wu[:, HEAD_DIM:]
        for c in range(nck):
            rs = slice(c * GDN_CHUNK, (c + 1) * GDN_CHUNK)
            wq_ref[2 * c * GDN_CHUNK:(2 * c + 1) * GDN_CHUNK, sl] = w[rs, :]
            wq_ref[(2 * c + 1) * GDN_CHUNK:(2 * c + 2) * GDN_CHUNK, sl] = qd[rs, :]
        kdt_ref[0, sl, :] = (kn * ekd_c).T.astype(BF16)


def _gdn_prep(proj, gates, conv_w, alog_row, dtb_row, batch, seq):
    tb = GDN_BLOCK
    nt = seq // tb
    bt = batch * seq
    width = HEADS * HEAD_DIM
    blk = lambda col: pl.BlockSpec((tb, width), lambda b, t: (b * nt + t, col))
    prev = lambda col: pl.BlockSpec((8, width), lambda b, t: (jnp.maximum((b * nt + t) * (tb // 8) - 1, 0), col))
    return pl.pallas_call(
        _gdn_prep_kernel,
        grid=(batch, nt),
        in_specs=[blk(0), blk(1), blk(2), prev(0), prev(1), prev(2),
                  pl.BlockSpec((tb, GATE_LANES), lambda b, t: (b * nt + t, 0)),
                  pl.BlockSpec((CONV_K, 3 * width), lambda b, t: (0, 0)),
                  pl.BlockSpec((1, GATE_LANES), lambda b, t: (0, 0)),
                  pl.BlockSpec((1, GATE_LANES), lambda b, t: (0, 0))],
        out_specs=[pl.BlockSpec((2 * tb, width), lambda b, t: (b * nt + t, 0)),
                   pl.BlockSpec((tb, width), lambda b, t: (b * nt + t, 0)),
                   pl.BlockSpec((1, width, tb), lambda b, t: (b * nt + t, 0, 0)),
                   pl.BlockSpec((tb, HEADS * tb), lambda b, t: (b * nt + t, 0)),
                   pl.BlockSpec((tb, GATE_LANES), lambda b, t: (b * nt + t, 0))],
        out_shape=[jax.ShapeDtypeStruct((2 * bt, width), BF16),
                   jax.ShapeDtypeStruct((bt, width), F32),
                   jax.ShapeDtypeStruct((bt // tb, width, tb), BF16),
                   jax.ShapeDtypeStruct((bt, HEADS * tb), BF16),
                   jax.ShapeDtypeStruct((bt, GATE_LANES), F32)],
        scratch_shapes=[pltpu.VMEM((3 * HEADS, tb + 8, HEAD_DIM), F32)],
        compiler_params=_params("parallel", "arbitrary"),
        name="gdn_prep",
    )(proj, proj, proj, proj, proj, proj, gates, conv_w, alog_row, dtb_row)


def _gdn_scan_kernel(wq_ref, u_ref, kdt_ref, attn_ref, egl_ref, z_ref, nw_ref, o_ref, s_ref, vn_ref):
    tb = GDN_BLOCK
    nck = tb // GDN_CHUNK
    ck = GDN_CHUNK

    @pl.when(pl.program_id(1) == 0)
    def _():
        s_ref[...] = jnp.zeros_like(s_ref)

    zeros_half = jnp.zeros((ck, HEAD_DIM), BF16)
    for h in range(HEADS):
        sl = slice(h * HEAD_DIM, (h + 1) * HEAD_DIM)
        state = s_ref[h]
        inter = []
        for c in range(nck):
            rs = slice(c * ck, (c + 1) * ck)
            r = jnp.dot(wq_ref[2 * c * ck:(2 * c + 2) * ck, sl], state.astype(BF16),
                        preferred_element_type=F32)
            vnew = (u_ref[rs, sl] - r[:ck, :]).astype(BF16)
            inter.append(r[ck:, :])
            vn_ref[h, rs, :] = vnew
            vpad = jnp.concatenate([vnew, zeros_half] if c % 2 == 0 else [zeros_half, vnew], axis=0)
            slab = kdt_ref[0, sl, (c // 2) * 2 * ck:(c // 2 + 1) * 2 * ck]
            cd = egl_ref[c * ck:c * ck + 1, HEADS + h:HEADS + h + 1]
            state = state * cd + jnp.dot(slab, vpad, preferred_element_type=F32)
        s_ref[h] = state
        o = jnp.concatenate(inter, axis=0) + jnp.dot(attn_ref[:, h * tb:(h + 1) * tb], vn_ref[h],
                                                     preferred_element_type=F32)
        o = o * lax.rsqrt(jnp.mean(o * o, axis=-1, keepdims=True) + RMS_EPS) * nw_ref[...]
        o_ref[:, sl] = (o * _silu(z_ref[:, sl])).astype(BF16)


def _gdn_scan(wq, u, kdt, attn, egl, proj, norm_w, batch, seq):
    tb = GDN_BLOCK
    nt = seq // tb
    bt = batch * seq
    width = HEADS * HEAD_DIM
    return pl.pallas_call(
        _gdn_scan_kernel,
        grid=(batch, nt),
        in_specs=[pl.BlockSpec((2 * tb, width), lambda b, t: (b * nt + t, 0)),
                  pl.BlockSpec((tb, width), lambda b, t: (b * nt + t, 0)),
                  pl.BlockSpec((1, width, tb), lambda b, t: (b * nt + t, 0, 0)),
                  pl.BlockSpec((tb, HEADS * tb), lambda b, t: (b * nt + t, 0)),
                  pl.BlockSpec((tb, GATE_LANES), lambda b, t: (b * nt + t, 0)),
                  pl.BlockSpec((tb, width), lambda b, t: (b * nt + t, 3)),
                  pl.BlockSpec((1, HEAD_DIM), lambda b, t: (0, 0))],
        out_specs=pl.BlockSpec((tb, width), lambda b, t: (b * nt + t, 0)),
        out_shape=jax.ShapeDtypeStruct((bt, width), BF16),
        scratch_shapes=[pltpu.VMEM((HEADS, HEAD_DIM, HEAD_DIM), F32),
                        pltpu.VMEM((HEADS, tb, HEAD_DIM), BF16)],
        compiler_params=_params("parallel", "arbitrary"),
        name="gdn_scan",
    )(wq, u, kdt, attn, egl, proj, norm_w)


def _gdn_layer(x, w_in, conv_w, a_log, dt_bias, norm_w, w_out, ln_g, ln_b, batch, seq):
    width = HEADS * HEAD_DIM
    w_main = w_in[:, :4 * width].astype(BF16)
    w_gate = jnp.pad(w_in[:, 4 * width:], ((0, 0), (0, GATE_LANES - 2 * HEADS))).astype(BF16)
    pad_row = lambda v: jnp.pad(v.astype(F32), (HEADS, GATE_LANES - 2 * HEADS))[None, :]
    proj = _matmul(x, w_main, F32, 512, 1024, "gdn_in_proj")
    gates = _matmul(x, w_gate, F32, 1024, GATE_LANES, "gdn_gate_proj")
    wq, u, kdt, attn, egl = _gdn_prep(proj, gates, conv_w.astype(F32), pad_row(a_log), pad_row(dt_bias), batch, seq)
    o = _gdn_scan(wq, u, kdt, attn, egl, proj, norm_w.astype(F32)[None, :], batch, seq)
    return _out_proj_ln(o, w_out.astype(BF16), x, ln_g[None, :], ln_b[None, :], 512, "gdn_out_ln")


def _kv_kernel(x_ref, w_ref, kaug_ref, vaug_ref, kmean_ref, *, nq):
    blk = MOBA_BLOCK
    width = HEADS * HEAD_DIM
    aug = 2 * HEAD_DIM
    nblk = x_ref.shape[0] // blk
    lane = lax.broadcasted_iota(jnp.int32, (blk, HEAD_DIM), 1)
    ones_col = (lane == 0).astype(BF16)
    for r in range(nblk):
        rs = slice(r * blk, (r + 1) * blk)
        acc = jnp.dot(x_ref[rs, :].astype(BF16), w_ref[...], preferred_element_type=F32)
        onehot = (lane == lax.rem(pl.program_id(0) * nblk + r, nq)).astype(BF16)
        for h in range(HEADS):
            kaug_ref[rs, h * aug:h * aug + HEAD_DIM] = acc[:, h * HEAD_DIM:(h + 1) * HEAD_DIM].astype(BF16)
            kaug_ref[rs, h * aug + HEAD_DIM:(h + 1) * aug] = onehot
            vaug_ref[rs, h * aug:h * aug + HEAD_DIM] = acc[:, width + h * HEAD_DIM:width + (h + 1) * HEAD_DIM].astype(BF16)
            vaug_ref[rs, h * aug + HEAD_DIM:(h + 1) * aug] = ones_col
        row = lax.rem(pl.program_id(0), 8 // nblk) * nblk + r
        kmean_ref[pl.ds(row, 1), :] = jnp.mean(acc[:, :width], axis=0, keepdims=True)


def _shared_kv(x, w_kv, seq):
    m, k = x.shape
    n = w_kv.shape[1]
    tm = 4 * MOBA_BLOCK
    steps_per_mean_block = 8 * MOBA_BLOCK // tm
    return pl.pallas_call(
        functools.partial(_kv_kernel, nq=seq // MOBA_BLOCK),
        grid=(m // tm,),
        in_specs=[pl.BlockSpec((tm, k), lambda i: (i, 0)),
                  pl.BlockSpec((k, n), lambda i: (0, 0))],
        out_specs=[pl.BlockSpec((tm, n), lambda i: (i, 0)),
                   pl.BlockSpec((tm, n), lambda i: (i, 0)),
                   pl.BlockSpec((8, n // 2), lambda i: (i // steps_per_mean_block, 0))],
        out_shape=[jax.ShapeDtypeStruct((m, n), BF16),
                   jax.ShapeDtypeStruct((m, n), BF16),
                   jax.ShapeDtypeStruct((m // MOBA_BLOCK, n // 2), F32)],
        compiler_params=_params("arbitrary"),
        name="moba_shared_kv",
    )(x, w_kv)


def _moba_select_bias(q_b, kmean, qb):
    nb = kmean.shape[0]
    blk = q_b.shape[0]
    km = jnp.concatenate([kmean, jnp.zeros((HEAD_DIM - nb, HEAD_DIM), F32)], axis=0).astype(BF16)
    gate_t = lax.dot_general(km, q_b, _NT, preferred_element_type=F32)[:nb]
    jrow = lax.broadcasted_iota(jnp.int32, (nb, blk), 0)
    valid = jrow < qb
    jf = jrow.astype(F32)
    g = jnp.where(valid, gate_t, -jnp.inf)
    keep = jnp.zeros((nb, blk), F32)
    for _ in range(MOBA_TOP_K):
        mx = jnp.max(g, axis=0, keepdims=True)
        first = jnp.min(jnp.where(g == mx, jf, float(nb)), axis=0, keepdims=True)
        sel = jf == first
        keep = jnp.where(sel, 1.0, keep)
        g = jnp.where(sel, -jnp.inf, g)
    bias_t = jnp.where(valid & (keep > 0.0), 0.0, MASK_VALUE)
    bias_t = jnp.concatenate([bias_t, jnp.zeros((HEAD_DIM - nb, blk), F32)], axis=0)
    return bias_t.T


def _moba_attn_kernel(q_ref, z_ref, k_ref, v_ref, kmean_ref, o_ref, acc_ref):
    blk = MOBA_BLOCK
    grp = MOBA_GROUP
    aug = 2 * HEAD_DIM
    nq = k_ref.shape[0] // blk
    qb = pl.program_id(2)
    own = pl.multiple_of(qb * blk, blk)
    qi = lax.broadcasted_iota(jnp.int32, (blk, blk), 0)
    ki = lax.broadcasted_iota(jnp.int32, (blk, blk), 1)
    causal = ki <= qi
    zeros_b = jnp.zeros((blk, HEAD_DIM), BF16)

    q_augs, m_init = [], []
    for h in range(MOBA_HEADS_PER_STEP):
        hs = slice(h * HEAD_DIM, (h + 1) * HEAD_DIM)
        ha = slice(h * aug, (h + 1) * aug)
        q_b = (q_ref[:, hs] * (HEAD_DIM ** -0.5)).astype(BF16)
        bias = _moba_select_bias(q_b, kmean_ref[:, hs], qb)
        q_aug = jnp.concatenate([q_b, bias.astype(BF16)], axis=1)
        q_own = jnp.concatenate([q_b, zeros_b], axis=1)
        s_own = lax.dot_general(q_own, k_ref[pl.ds(own, blk), ha], _NT, preferred_element_type=F32)
        s_own = jnp.where(causal, s_own, MASK_VALUE)
        s_rest = lax.dot_general(q_aug, k_ref[0:(grp - 1) * blk, ha], _NT, preferred_element_type=F32)
        m = jnp.maximum(jnp.max(s_own, axis=-1, keepdims=True), jnp.max(s_rest, axis=-1, keepdims=True))
        acc_ref[h] = (jnp.dot(jnp.exp(s_own - m).astype(BF16), v_ref[pl.ds(own, blk), ha],
                              preferred_element_type=F32)
                      + jnp.dot(jnp.exp(s_rest - m).astype(BF16), v_ref[0:(grp - 1) * blk, ha],
                                preferred_element_type=F32))
        q_augs.append(q_aug)
        m_init.append(m)

    def body(i, ms):
        start = pl.multiple_of((grp - 1 + i * grp) * blk, blk)
        out = []
        for h in range(MOBA_HEADS_PER_STEP):
            ha = slice(h * aug, (h + 1) * aug)
            s = lax.dot_general(q_augs[h], k_ref[pl.ds(start, grp * blk), ha], _NT, preferred_element_type=F32)
            m_new = jnp.maximum(ms[h], jnp.max(s, axis=-1, keepdims=True))
            p = jnp.exp(s - m_new).astype(BF16)
            acc_ref[h] = jnp.exp(ms[h] - m_new) * acc_ref[h] + jnp.dot(
                p, v_ref[pl.ds(start, grp * blk), ha], preferred_element_type=F32)
            out.append(m_new)
        return tuple(out)

    n_groups = (jnp.maximum(qb - (grp - 1), 0) + grp - 1) // grp
    lax.fori_loop(0, n_groups, body, tuple(m_init))
    for h in range(MOBA_HEADS_PER_STEP):
        hs = slice(h * HEAD_DIM, (h + 1) * HEAD_DIM)
        acc = acc_ref[h]
        o = acc[:, :HEAD_DIM] / acc[:, HEAD_DIM:HEAD_DIM + 1]
        o_ref[:, hs] = (o * _silu(z_ref[:, hs])).astype(BF16)


def _moba_attn(qz, kaug, vaug, kmean, batch, seq):
    blk = MOBA_BLOCK
    nq = seq // blk
    bt = batch * seq
    hb = MOBA_HEADS_PER_STEP
    pairs = HEADS // hb
    assert nq % MOBA_GROUP == 0
    return pl.pallas_call(
        _moba_attn_kernel,
        grid=(batch, pairs, nq),
        in_specs=[pl.BlockSpec((blk, hb * HEAD_DIM), lambda b, h, i: (b * nq + i, h)),
                  pl.BlockSpec((blk, hb * HEAD_DIM), lambda b, h, i: (b * nq + i, pairs + h)),
                  pl.BlockSpec((seq, hb * 2 * HEAD_DIM), lambda b, h, i: (b, h)),
                  pl.BlockSpec((seq, hb * 2 * HEAD_DIM), lambda b, h, i: (b, h)),
                  pl.BlockSpec((nq, hb * HEAD_DIM), lambda b, h, i: (b, h))],
        out_specs=pl.BlockSpec((blk, hb * HEAD_DIM), lambda b, h, i: (b * nq + i, h)),
        out_shape=jax.ShapeDtypeStruct((bt, HEADS * HEAD_DIM), BF16),
        scratch_shapes=[pltpu.VMEM((hb, blk, 2 * HEAD_DIM), F32)],
        compiler_params=_params("parallel", "parallel", "arbitrary"),
        name="moba_attn",
    )(qz, qz, kaug, vaug, kmean)


def _moba_layer(x, w_in, w_out, ln_g, ln_b, kaug, vaug, kmean, batch, seq):
    qz = _matmul(x, w_in.astype(BF16), F32, 512, 1024, "moba_in_proj")
    o = _moba_attn(qz, kaug, vaug, kmean, batch, seq)
    return _out_proj_ln(o, w_out.astype(BF16), x, ln_g[None, :], ln_b[None, :], 512, "moba_out_ln")


def kernel(x, a_w_in, a_conv_w, a_A_log, a_dt_bias, a_norm_w, a_w_out, a_ln_g, a_ln_b, b_w_kv, b_w_in, b_w_out, b_ln_g, b_ln_b):
    batch, seq, d_model = x.shape
    assert d_model == D_MODEL and seq % (MOBA_GROUP * MOBA_BLOCK) == 0 and seq // MOBA_BLOCK <= HEAD_DIM
    assert a_w_in.shape[2] == 4 * HEADS * HEAD_DIM + 2 * HEADS and a_conv_w.shape[1] == CONV_K
    h = x.reshape(batch * seq, d_model)
    for i in range(a_w_in.shape[0]):
        h = _gdn_layer(h, a_w_in[i], a_conv_w[i], a_A_log[i], a_dt_bias[i], a_norm_w[i], a_w_out[i],
                       a_ln_g[i], a_ln_b[i], batch, seq)
    kaug, vaug, kmean = _shared_kv(h, b_w_kv.astype(BF16), seq)
    for i in range(b_w_in.shape[0]):
        h = _moba_layer(h, b_w_in[i], b_w_out[i], b_ln_g[i], b_ln_b[i], kaug, vaug, kmean, batch, seq)
    return h.reshape(batch, seq, d_model)
```

```python
import functools

import jax
import jax.numpy as jnp
from jax import lax
from jax.experimental import pallas as pl
from jax.experimental.pallas import tpu as pltpu

F32 = jnp.float32
BF16 = jnp.bfloat16

D_MODEL = 1024
HEADS = 8
HEAD_DIM = 128
CONV_K = 4
GDN_CHUNK = 64
GDN_BLOCK = 256
MOBA_BLOCK = 256
MOBA_TOP_K = 3
MOBA_GROUP = 8
MOBA_HEADS_PER_STEP = 2
DEPTH = 4
DEEPNORM_ALPHA = (2 * DEPTH) ** 0.25
LN_EPS = 1e-5
RMS_EPS = 1e-6
MASK_VALUE = -1e30
GATE_LANES = 128
V7X_VMEM_LIMIT = 56 * 1024 * 1024
PROJ_COL_TILE = 1024

_NT = (((1,), (1,)), ((), ()))


def _params(*sem):
    return pltpu.CompilerParams(dimension_semantics=sem, vmem_limit_bytes=V7X_VMEM_LIMIT)


def _silu(x):
    return x * jax.nn.sigmoid(x)


def _proj_kernel(x_ref, *refs):
    nw = len(refs) // 2
    xb = x_ref[...].astype(BF16)
    for w_ref, o_ref in zip(refs[:nw], refs[nw:]):
        n = w_ref.shape[1]
        tn = min(n, PROJ_COL_TILE)
        for c in range(n // tn):
            o_ref[:, c * tn:(c + 1) * tn] = jnp.dot(xb, w_ref[:, c * tn:(c + 1) * tn],
                                                    preferred_element_type=F32).astype(o_ref.dtype)


def _project(x, weights, out_dtypes, tm, name):
    m, k = x.shape
    return pl.pallas_call(
        _proj_kernel,
        grid=(m // tm,),
        in_specs=[pl.BlockSpec((tm, k), lambda i: (i, 0))]
        + [pl.BlockSpec(w.shape, lambda i: (0, 0)) for w in weights],
        out_specs=[pl.BlockSpec((tm, w.shape[1]), lambda i: (i, 0)) for w in weights],
        out_shape=[jax.ShapeDtypeStruct((m, w.shape[1]), dt) for w, dt in zip(weights, out_dtypes)],
        compiler_params=_params("parallel"),
        name=name,
    )(x, *weights)


def _out_ln_kernel(o_ref, w_ref, x_ref, g_ref, b_ref, out_ref, y_ref):
    y_ref[...] = jnp.dot(o_ref[...], w_ref[...], preferred_element_type=F32)
    rows = 64
    for r in range(o_ref.shape[0] // rows):
        rs = slice(r * rows, (r + 1) * rows)
        s = DEEPNORM_ALPHA * x_ref[rs, :] + y_ref[rs, :]
        mu = jnp.mean(s, axis=-1, keepdims=True)
        c = s - mu
        var = jnp.mean(c * c, axis=-1, keepdims=True)
        out_ref[rs, :] = c * lax.rsqrt(var + LN_EPS) * g_ref[...] + b_ref[...]


def _out_proj_ln(o, w, x, g, b, tm, name):
    m, k = o.shape
    n = w.shape[1]
    return pl.pallas_call(
        _out_ln_kernel,
        grid=(m // tm,),
        in_specs=[pl.BlockSpec((tm, k), lambda i: (i, 0)),
                  pl.BlockSpec((k, n), lambda i: (0, 0)),
                  pl.BlockSpec((tm, n), lambda i: (i, 0)),
                  pl.BlockSpec((1, n), lambda i: (0, 0)),
                  pl.BlockSpec((1, n), lambda i: (0, 0))],
        out_specs=pl.BlockSpec((tm, n), lambda i: (i, 0)),
        out_shape=jax.ShapeDtypeStruct((m, n), F32),
        scratch_shapes=[pltpu.VMEM((tm, n), F32)],
        compiler_params=_params("parallel"),
        name=name,
    )(o, w, x, g, b)


def _gdn_prep_kernel(q_ref, k_ref, v_ref, qp_ref, kp_ref, vp_ref, gates_ref, cw_ref, alog_ref, dtb_ref,
                     wq_ref, u_ref, kdt_ref, attn_ref, egl_ref, xs_ref, p_ref, tm_ref, rhs_ref):
    tb = GDN_BLOCK
    nck = tb // GDN_CHUNK
    first = pl.program_id(1) == 0

    gts = gates_ref[...]
    beta_all = jax.nn.sigmoid(gts)
    xg = gts + dtb_ref[...]
    softplus = jnp.maximum(xg, 0.0) + jnp.log1p(jnp.exp(-jnp.abs(xg)))
    g_all = -jnp.exp(alog_ref[...]) * softplus
    row = lax.broadcasted_iota(jnp.int32, (tb, GATE_LANES), 0)
    rin = jnp.bitwise_and(row, GDN_CHUNK - 1)
    gc = g_all
    s = 1
    while s < GDN_CHUNK:
        gc = gc + jnp.where(rin >= s, pltpu.roll(gc, s, axis=0), 0.0)
        s *= 2
    gc3 = gc.reshape(nck, GDN_CHUNK, GATE_LANES)
    gl = jnp.broadcast_to(gc3[:, GDN_CHUNK - 1:GDN_CHUNK, :], gc3.shape).reshape(tb, GATE_LANES)
    egc = jnp.exp(gc)
    ekd = jnp.exp(gl - gc)
    egl_ref[...] = jnp.exp(gl)
    gct = gc.T

    ri = lax.broadcasted_iota(jnp.int32, (tb, tb), 0)
    ci = lax.broadcasted_iota(jnp.int32, (tb, tb), 1)
    same = jnp.right_shift(ri, 6) == jnp.right_shift(ci, 6)
    causal = same & (ri >= ci)
    strict = same & (ri > ci)
    eye = (ri == ci).astype(F32)
    eye_b = eye.astype(BF16)

    def conv_silu(x_ref, p_ref, slot, h, col0):
        sl = slice(h * HEAD_DIM, (h + 1) * HEAD_DIM)
        xs_ref[slot, 0:8, :] = jnp.where(first, 0.0, p_ref[:, sl])
        xs_ref[slot, 8:8 + tb, :] = x_ref[:, sl]
        cw = cw_ref[:, col0 + h * HEAD_DIM:col0 + (h + 1) * HEAD_DIM]
        acc = xs_ref[slot, 8:8 + tb, :] * cw[CONV_K - 1:CONV_K, :]
        for d in range(1, CONV_K):
            acc = acc + xs_ref[slot, 8 - d:8 - d + tb, :] * cw[CONV_K - 1 - d:CONV_K - d, :]
        return _silu(acc)

    for h in range(HEADS):
        sl = slice(h * HEAD_DIM, (h + 1) * HEAD_DIM)
        qc = conv_silu(q_ref, qp_ref, 3 * h, h, 0)
        kc = conv_silu(k_ref, kp_ref, 3 * h + 1, h, HEADS * HEAD_DIM)
        vc = conv_silu(v_ref, vp_ref, 3 * h + 2, h, 2 * HEADS * HEAD_DIM)
        qn = qc * lax.rsqrt(jnp.sum(qc * qc, axis=-1, keepdims=True) + RMS_EPS)
        kn = kc * lax.rsqrt(jnp.sum(kc * kc, axis=-1, keepdims=True) + RMS_EPS)
        beta = beta_all[:, h:h + 1]
        gcol = gc[:, HEADS + h:HEADS + h + 1]
        egc_c = egc[:, HEADS + h:HEADS + h + 1]
        ekd_c = ekd[:, HEADS + h:HEADS + h + 1]
        grow = gct[HEADS + h:HEADS + h + 1, :]

        qs = qn * (HEAD_DIM ** -0.5)
        kb = kn * beta
        kn_b = kn.astype(BF16)
        kk = lax.dot_general(kb.astype(BF16), kn_b, _NT, preferred_element_type=F32)
        qk = lax.dot_general(qs.astype(BF16), kn_b, _NT, preferred_element_type=F32)
        dm = jnp.exp(jnp.where(causal, gcol - grow, -jnp.inf))
        attn_ref[:, h * tb:(h + 1) * tb] = (qk * dm).astype(BF16)
        nm = jnp.where(strict, -(kk * dm), 0.0)
        p_ref[h] = nm.astype(BF16)
        tm_ref[h] = (eye + nm).astype(BF16)
        rhs_ref[h, :, :HEAD_DIM] = (kb * egc_c).astype(BF16)
        rhs_ref[h, :, HEAD_DIM:] = (vc * beta).astype(BF16)
        qd = (qs * egc_c).astype(BF16)
        for c in range(nck):
            wq_ref[(2 * c + 1) * GDN_CHUNK:(2 * c + 2) * GDN_CHUNK, sl] = qd[c * GDN_CHUNK:(c + 1) * GDN_CHUNK, :]
        kdt_ref[0, sl, :] = (kn * ekd_c).T.astype(BF16)

    for _ in range(GDN_CHUNK.bit_length() - 2):
        for h in range(HEADS):
            p = p_ref[h]
            p_ref[h] = jnp.dot(p, p, preferred_element_type=F32).astype(BF16)
        for h in range(HEADS):
            tm_ref[h] = jnp.dot(tm_ref[h], p_ref[h] + eye_b, preferred_element_type=F32).astype(BF16)
    for h in range(HEADS):
        sl = slice(h * HEAD_DIM, (h + 1) * HEAD_DIM)
        wu = jnp.dot(tm_ref[h], rhs_ref[h], preferred_element_type=F32)
        w = wu[:, :HEAD_DIM].astype(BF16)
        u_ref[:, sl] = wu[:, HEAD_DIM:]
        for c in range(nck):
            wq_ref[2 * c * GDN_CHUNK:(2 * c + 1) * GDN_CHUNK, sl] = w[c * GDN_CHUNK:(c + 1) * GDN_CHUNK, :]


def _gdn_prep(proj, gates, conv_w, alog_row, dtb_row, batch, seq):
    tb = GDN_BLOCK
    nt = seq // tb
    bt = batch * seq
    width = HEADS * HEAD_DIM
    blk = lambda col: pl.BlockSpec((tb, width), lambda b, t: (b * nt + t, col))
    prev = lambda col: pl.BlockSpec((8, width), lambda b, t: (jnp.maximum((b * nt + t) * (tb // 8) - 1, 0), col))
    return pl.pallas_call(
        _gdn_prep_kernel,
        grid=(batch, nt),
        in_specs=[blk(0), blk(1), blk(2), prev(0), prev(1), prev(2),
                  pl.BlockSpec((tb, GATE_LANES), lambda b, t: (b * nt + t, 0)),
                  pl.BlockSpec((CONV_K, 3 * width), lambda b, t: (0, 0)),
                  pl.BlockSpec((1, GATE_LANES), lambda b, t: (0, 0)),
                  pl.BlockSpec((1, GATE_LANES), lambda b, t: (0, 0))],
        out_specs=[pl.BlockSpec((2 * tb, width), lambda b, t: (b * nt + t, 0)),
                   pl.BlockSpec((tb, width), lambda b, t: (b * nt + t, 0)),
                   pl.BlockSpec((1, width, tb), lambda b, t: (b * nt + t, 0, 0)),
                   pl.BlockSpec((tb, HEADS * tb), lambda b, t: (b * nt + t, 0)),
                   pl.BlockSpec((tb, GATE_LANES), lambda b, t: (b * nt + t, 0))],
        out_shape=[jax.ShapeDtypeStruct((2 * bt, width), BF16),
                   jax.ShapeDtypeStruct((bt, width), F32),
                   jax.ShapeDtypeStruct((bt // tb, width, tb), BF16),
                   jax.ShapeDtypeStruct((bt, HEADS * tb), BF16),
                   jax.ShapeDtypeStruct((bt, GATE_LANES), F32)],
        scratch_shapes=[pltpu.VMEM((3 * HEADS, tb + 8, HEAD_DIM), F32),
                        pltpu.VMEM((HEADS, tb, tb), BF16),
                        pltpu.VMEM((HEADS, tb, tb), BF16),
                        pltpu.VMEM((HEADS, tb, 2 * HEAD_DIM), BF16)],
        compiler_params=_params("parallel", "arbitrary"),
        name="gdn_prep",
    )(proj, proj, proj, proj, proj, proj, gates, conv_w, alog_row, dtb_row)


def _gdn_scan_kernel(wq_ref, u_ref, kdt_ref, attn_ref, egl_ref, z_ref, nw_ref, o_ref, s_ref, vn_ref, qs_ref):
    tb = GDN_BLOCK
    nck = tb // GDN_CHUNK
    ck = GDN_CHUNK

    @pl.when(pl.program_id(1) == 0)
    def _():
        s_ref[...] = jnp.zeros_like(s_ref)

    zeros_half = jnp.zeros((ck, HEAD_DIM), BF16)
    for c in range(nck):
        rs = slice(c * ck, (c + 1) * ck)
        for h in range(HEADS):
            sl = slice(h * HEAD_DIM, (h + 1) * HEAD_DIM)
            state = s_ref[h]
            r = jnp.dot(wq_ref[2 * c * ck:(2 * c + 2) * ck, sl], state.astype(BF16),
                        preferred_element_type=F32)
            vnew = (u_ref[rs, sl] - r[:ck, :]).astype(BF16)
            qs_ref[h, rs, :] = r[ck:, :]
            vn_ref[h, rs, :] = vnew
            vpad = jnp.concatenate([vnew, zeros_half] if c % 2 == 0 else [zeros_half, vnew], axis=0)
            slab = kdt_ref[0, sl, (c // 2) * 2 * ck:(c // 2 + 1) * 2 * ck]
            cd = egl_ref[c * ck:c * ck + 1, HEADS + h:HEADS + h + 1]
            s_ref[h] = state * cd + jnp.dot(slab, vpad, preferred_element_type=F32)
    for h in range(HEADS):
        sl = slice(h * HEAD_DIM, (h + 1) * HEAD_DIM)
        o = qs_ref[h] + jnp.dot(attn_ref[:, h * tb:(h + 1) * tb], vn_ref[h], preferred_element_type=F32)
        o = o * lax.rsqrt(jnp.mean(o * o, axis=-1, keepdims=True) + RMS_EPS) * nw_ref[...]
        o_ref[:, sl] = (o * _silu(z_ref[:, sl])).astype(BF16)


def _gdn_scan(wq, u, kdt, attn, egl, proj, norm_w, batch, seq):
    tb = GDN_BLOCK
    nt = seq // tb
    bt = batch * seq
    width = HEADS * HEAD_DIM
    return pl.pallas_call(
        _gdn_scan_kernel,
        grid=(batch, nt),
        in_specs=[pl.BlockSpec((2 * tb, width), lambda b, t: (b * nt + t, 0)),
                  pl.BlockSpec((tb, width), lambda b, t: (b * nt + t, 0)),
                  pl.BlockSpec((1, width, tb), lambda b, t: (b * nt + t, 0, 0)),
                  pl.BlockSpec((tb, HEADS * tb), lambda b, t: (b * nt + t, 0)),
                  pl.BlockSpec((tb, GATE_LANES), lambda b, t: (b * nt + t, 0)),
                  pl.BlockSpec((tb, width), lambda b, t: (b * nt + t, 3)),
                  pl.BlockSpec((1, HEAD_DIM), lambda b, t: (0, 0))],
        out_specs=pl.BlockSpec((tb, width), lambda b, t: (b * nt + t, 0)),
        out_shape=jax.ShapeDtypeStruct((bt, width), BF16),
        scratch_shapes=[pltpu.VMEM((HEADS, HEAD_DIM, HEAD_DIM), F32),
                        pltpu.VMEM((HEADS, tb, HEAD_DIM), BF16),
                        pltpu.VMEM((HEADS, tb, HEAD_DIM), F32)],
        compiler_params=_params("parallel", "arbitrary"),
        name="gdn_scan",
    )(wq, u, kdt, attn, egl, proj, norm_w)


def _gdn_layer(x, w_in, conv_w, a_log, dt_bias, norm_w, w_out, ln_g, ln_b, batch, seq):
    width = HEADS * HEAD_DIM
    w_main = w_in[:, :4 * width].astype(BF16)
    w_gate = jnp.pad(w_in[:, 4 * width:], ((0, 0), (0, GATE_LANES - 2 * HEADS))).astype(BF16)
    pad_row = lambda v: jnp.pad(v.astype(F32), (HEADS, GATE_LANES - 2 * HEADS))[None, :]
    proj, gates = _project(x, [w_main, w_gate], [F32, F32], 512, "gdn_in_proj")
    wq, u, kdt, attn, egl = _gdn_prep(proj, gates, conv_w.astype(F32), pad_row(a_log), pad_row(dt_bias), batch, seq)
    o = _gdn_scan(wq, u, kdt, attn, egl, proj, norm_w.astype(F32)[None, :], batch, seq)
    return _out_proj_ln(o, w_out.astype(BF16), x, ln_g[None, :], ln_b[None, :], 512, "gdn_out_ln")


def _kv_kernel(x_ref, w_ref, kaug_ref, vaug_ref, kmean_ref, *, nq):
    blk = MOBA_BLOCK
    width = HEADS * HEAD_DIM
    aug = 2 * HEAD_DIM
    nblk = x_ref.shape[0] // blk
    lane = lax.broadcasted_iota(jnp.int32, (blk, HEAD_DIM), 1)
    ones_col = (lane == 0).astype(BF16)
    for r in range(nblk):
        rs = slice(r * blk, (r + 1) * blk)
        acc = jnp.dot(x_ref[rs, :].astype(BF16), w_ref[...], preferred_element_type=F32)
        onehot = (lane == lax.rem(pl.program_id(0) * nblk + r, nq)).astype(BF16)
        for h in range(HEADS):
            kaug_ref[rs, h * aug:h * aug + HEAD_DIM] = acc[:, h * HEAD_DIM:(h + 1) * HEAD_DIM].astype(BF16)
            kaug_ref[rs, h * aug + HEAD_DIM:(h + 1) * aug] = onehot
            vaug_ref[rs, h * aug:h * aug + HEAD_DIM] = acc[:, width + h * HEAD_DIM:width + (h + 1) * HEAD_DIM].astype(BF16)
            vaug_ref[rs, h * aug + HEAD_DIM:(h + 1) * aug] = ones_col
        row = lax.rem(pl.program_id(0), 8 // nblk) * nblk + r
        kmean_ref[pl.ds(row, 1), :] = jnp.mean(acc[:, :width], axis=0, keepdims=True)


def _shared_kv(x, w_kv, seq):
    m, k = x.shape
    n = w_kv.shape[1]
    tm = 4 * MOBA_BLOCK
    steps_per_mean_block = 8 * MOBA_BLOCK // tm
    return pl.pallas_call(
        functools.partial(_kv_kernel, nq=seq // MOBA_BLOCK),
        grid=(m // tm,),
        in_specs=[pl.BlockSpec((tm, k), lambda i: (i, 0)),
                  pl.BlockSpec((k, n), lambda i: (0, 0))],
        out_specs=[pl.BlockSpec((tm, n), lambda i: (i, 0)),
                   pl.BlockSpec((tm, n), lambda i: (i, 0)),
                   pl.BlockSpec((8, n // 2), lambda i: (i // steps_per_mean_block, 0))],
        out_shape=[jax.ShapeDtypeStruct((m, n), BF16),
                   jax.ShapeDtypeStruct((m, n), BF16),
                   jax.ShapeDtypeStruct((m // MOBA_BLOCK, n // 2), F32)],
        compiler_params=_params("arbitrary"),
        name="moba_shared_kv",
    )(x, w_kv)


def _moba_select_bias(q_b, kmean, qb):
    nb = kmean.shape[0]
    blk = q_b.shape[0]
    km = jnp.concatenate([kmean, jnp.zeros((HEAD_DIM - nb, HEAD_DIM), F32)], axis=0).astype(BF16)
    gate_t = lax.dot_general(km, q_b, _NT, preferred_element_type=F32)[:nb]
    jrow = lax.broadcasted_iota(jnp.int32, (nb, blk), 0)
    valid = jrow < qb
    jf = jrow.astype(F32)
    g = jnp.where(valid, gate_t, -jnp.inf)
    keep = jnp.zeros((nb, blk), F32)
    for _ in range(MOBA_TOP_K):
        mx = jnp.max(g, axis=0, keepdims=True)
        first = jnp.min(jnp.where(g == mx, jf, float(nb)), axis=0, keepdims=True)
        sel = jf == first
        keep = jnp.where(sel, 1.0, keep)
        g = jnp.where(sel, -jnp.inf, g)
    bias_t = jnp.where(valid & (keep > 0.0), 0.0, MASK_VALUE)
    bias_t = jnp.concatenate([bias_t, jnp.zeros((HEAD_DIM - nb, blk), F32)], axis=0)
    return bias_t.T


def _moba_attn_kernel(q_ref, z_ref, k_ref, v_ref, kmean_ref, o_ref, acc_ref):
    blk = MOBA_BLOCK
    grp = MOBA_GROUP
    aug = 2 * HEAD_DIM
    nq = k_ref.shape[0] // blk
    qb = pl.program_id(2)
    own = pl.multiple_of(qb * blk, blk)
    qi = lax.broadcasted_iota(jnp.int32, (blk, blk), 0)
    ki = lax.broadcasted_iota(jnp.int32, (blk, blk), 1)
    causal = ki <= qi
    zeros_b = jnp.zeros((blk, HEAD_DIM), BF16)

    heads = range(MOBA_HEADS_PER_STEP)
    ha = [slice(h * aug, (h + 1) * aug) for h in heads]
    q_bs = [(q_ref[:, h * HEAD_DIM:(h + 1) * HEAD_DIM] * (HEAD_DIM ** -0.5)).astype(BF16) for h in heads]
    q_augs = [jnp.concatenate([q_bs[h], _moba_select_bias(
        q_bs[h], kmean_ref[:, h * HEAD_DIM:(h + 1) * HEAD_DIM], qb).astype(BF16)], axis=1) for h in heads]
    s_own = [jnp.where(causal, lax.dot_general(jnp.concatenate([q_bs[h], zeros_b], axis=1),
                                               k_ref[pl.ds(own, blk), ha[h]], _NT, preferred_element_type=F32),
                       MASK_VALUE) for h in heads]
    s_rest = [lax.dot_general(q_augs[h], k_ref[0:(grp - 1) * blk, ha[h]], _NT, preferred_element_type=F32)
              for h in heads]
    m_init = [jnp.maximum(jnp.max(s_own[h], axis=-1, keepdims=True), jnp.max(s_rest[h], axis=-1, keepdims=True))
              for h in heads]
    for h in heads:
        acc_ref[h] = (jnp.dot(jnp.exp(s_own[h] - m_init[h]).astype(BF16), v_ref[pl.ds(own, blk), ha[h]],
                              preferred_element_type=F32)
                      + jnp.dot(jnp.exp(s_rest[h] - m_init[h]).astype(BF16), v_ref[0:(grp - 1) * blk, ha[h]],
                                preferred_element_type=F32))

    def body(i, ms):
        start = pl.multiple_of((grp - 1 + i * grp) * blk, blk)
        s = [lax.dot_general(q_augs[h], k_ref[pl.ds(start, grp * blk), ha[h]], _NT, preferred_element_type=F32)
             for h in heads]
        m_new = [jnp.maximum(ms[h], jnp.max(s[h], axis=-1, keepdims=True)) for h in heads]
        p = [jnp.exp(s[h] - m_new[h]).astype(BF16) for h in heads]
        for h in heads:
            acc_ref[h] = jnp.exp(ms[h] - m_new[h]) * acc_ref[h] + jnp.dot(
                p[h], v_ref[pl.ds(start, grp * blk), ha[h]], preferred_element_type=F32)
        return tuple(m_new)

    n_groups = (jnp.maximum(qb - (grp - 1), 0) + grp - 1) // grp
    lax.fori_loop(0, n_groups, body, tuple(m_init))
    for h in range(MOBA_HEADS_PER_STEP):
        hs = slice(h * HEAD_DIM, (h + 1) * HEAD_DIM)
        acc = acc_ref[h]
        o = acc[:, :HEAD_DIM] / acc[:, HEAD_DIM:HEAD_DIM + 1]
        o_ref[:, hs] = (o * _silu(z_ref[:, hs])).astype(BF16)


def _moba_attn(qz, kaug, vaug, kmean, batch, seq):
    blk = MOBA_BLOCK
    nq = seq // blk
    bt = batch * seq
    hb = MOBA_HEADS_PER_STEP
    pairs = HEADS // hb
    assert nq % MOBA_GROUP == 0
    return pl.pallas_call(
        _moba_attn_kernel,
        grid=(batch, pairs, nq),
        in_specs=[pl.BlockSpec((blk, hb * HEAD_DIM), lambda b, h, i: (b * nq + i, h)),
                  pl.BlockSpec((blk, hb * HEAD_DIM), lambda b, h, i: (b * nq + i, pairs + h)),
                  pl.BlockSpec((seq, hb * 2 * HEAD_DIM), lambda b, h, i: (b, h)),
                  pl.BlockSpec((seq, hb * 2 * HEAD_DIM), lambda b, h, i: (b, h)),
                  pl.BlockSpec((nq, hb * HEAD_DIM), lambda b, h, i: (b, h))],
        out_specs=pl.BlockSpec((blk, hb * HEAD_DIM), lambda b, h, i: (b * nq + i, h)),
        out_shape=jax.ShapeDtypeStruct((bt, HEADS * HEAD_DIM), BF16),
        scratch_shapes=[pltpu.VMEM((hb, blk, 2 * HEAD_DIM), F32)],
        compiler_params=_params("parallel", "parallel", "arbitrary"),
        name="moba_attn",
    )(qz, qz, kaug, vaug, kmean)


def _moba_layer(x, w_in, w_out, ln_g, ln_b, kaug, vaug, kmean, batch, seq):
    (qz,) = _project(x, [w_in.astype(BF16)], [F32], 1024, "moba_in_proj")
    o = _moba_attn(qz, kaug, vaug, kmean, batch, seq)
    return _out_proj_ln(o, w_out.astype(BF16), x, ln_g[None, :], ln_b[None, :], 512, "moba_out_ln")


def kernel(x, a_w_in, a_conv_w, a_A_log, a_dt_bias, a_norm_w, a_w_out, a_ln_g, a_ln_b, b_w_kv, b_w_in, b_w_out, b_ln_g, b_ln_b):
    batch, seq, d_model = x.shape
    assert d_model == D_MODEL and seq % (MOBA_GROUP * MOBA_BLOCK) == 0 and seq // MOBA_BLOCK <= HEAD_DIM
    assert a_w_in.shape[2] == 4 * HEADS * HEAD_DIM + 2 * HEADS and a_conv_w.shape[1] == CONV_K
    h = x.reshape(batch * seq, d_model)
    for i in range(a_w_in.shape[0]):
        h = _gdn_layer(h, a_w_in[i], a_conv_w[i], a_A_log[i], a_dt_bias[i], a_norm_w[i], a_w_out[i],
                       a_ln_g[i], a_ln_b[i], batch, seq)
    kaug, vaug, kmean = _shared_kv(h, b_w_kv.astype(BF16), seq)
    for i in range(b_w_in.shape[0]):
        h = _moba_layer(h, b_w_in[i], b_w_out[i], b_ln_g[i], b_ln_b[i], kaug, vaug, kmean, batch, seq)
    return h.reshape(batch, seq, d_model)
```

```python
import functools

import jax
import jax.numpy as jnp
from jax import lax
from jax.experimental import pallas as pl
from jax.experimental.pallas import tpu as pltpu

F32 = jnp.float32
BF16 = jnp.bfloat16

D_MODEL = 1024
HEADS = 8
HEAD_DIM = 128
CONV_K = 4
GDN_CHUNK = 64
GDN_BLOCK = 256
MOBA_BLOCK = 256
MOBA_TOP_K = 3
MOBA_GROUP = 4
MOBA_HEADS_PER_STEP = 4
DEPTH = 4
DEEPNORM_ALPHA = (2 * DEPTH) ** 0.25
LN_EPS = 1e-5
RMS_EPS = 1e-6
MASK_VALUE = -1e30
GATE_LANES = 128
V7X_VMEM_LIMIT = 56 * 1024 * 1024
PROJ_COL_TILE = 1024
OUT_LN_PARTS = 4

_NT = (((1,), (1,)), ((), ()))


def _params(*sem):
    return pltpu.CompilerParams(dimension_semantics=sem, vmem_limit_bytes=V7X_VMEM_LIMIT)


def _silu(x):
    return x * jax.nn.sigmoid(x)


def _proj_kernel(x_ref, *refs):
    nw = len(refs) // 2
    xb = x_ref[...].astype(BF16)
    for w_ref, o_ref in zip(refs[:nw], refs[nw:]):
        n = w_ref.shape[1]
        tn = min(n, PROJ_COL_TILE)
        for c in range(n // tn):
            o_ref[:, c * tn:(c + 1) * tn] = jnp.dot(xb, w_ref[:, c * tn:(c + 1) * tn],
                                                    preferred_element_type=F32).astype(o_ref.dtype)


def _project(x, weights, out_dtypes, tm, name):
    m, k = x.shape
    return pl.pallas_call(
        _proj_kernel,
        grid=(m // tm,),
        in_specs=[pl.BlockSpec((tm, k), lambda i: (i, 0))]
        + [pl.BlockSpec(w.shape, lambda i: (0, 0)) for w in weights],
        out_specs=[pl.BlockSpec((tm, w.shape[1]), lambda i: (i, 0)) for w in weights],
        out_shape=[jax.ShapeDtypeStruct((m, w.shape[1]), dt) for w, dt in zip(weights, out_dtypes)],
        compiler_params=_params("parallel"),
        name=name,
    )(x, *weights)


def _out_ln_kernel(o_ref, w_ref, x_ref, g_ref, b_ref, out_ref, y_ref):
    parts = OUT_LN_PARTS
    size = o_ref.shape[0] // parts
    rows = 64

    def matmul_part(i):
        y_ref[i * size:(i + 1) * size, :] = jnp.dot(o_ref[i * size:(i + 1) * size, :], w_ref[...],
                                                    preferred_element_type=F32)

    def norm_part(i):
        for r in range(i * size // rows, (i + 1) * size // rows):
            rs = slice(r * rows, (r + 1) * rows)
            s = DEEPNORM_ALPHA * x_ref[rs, :] + y_ref[rs, :]
            mu = jnp.mean(s, axis=-1, keepdims=True)
            c = s - mu
            var = jnp.mean(c * c, axis=-1, keepdims=True)
            out_ref[rs, :] = c * lax.rsqrt(var + LN_EPS) * g_ref[...] + b_ref[...]

    matmul_part(0)
    for i in range(parts):
        if i + 1 < parts:
            matmul_part(i + 1)
        norm_part(i)


def _out_proj_ln(o, w, x, g, b, tm, name):
    m, k = o.shape
    n = w.shape[1]
    return pl.pallas_call(
        _out_ln_kernel,
        grid=(m // tm,),
        in_specs=[pl.BlockSpec((tm, k), lambda i: (i, 0)),
                  pl.BlockSpec((k, n), lambda i: (0, 0)),
                  pl.BlockSpec((tm, n), lambda i: (i, 0)),
                  pl.BlockSpec((1, n), lambda i: (0, 0)),
                  pl.BlockSpec((1, n), lambda i: (0, 0))],
        out_specs=pl.BlockSpec((tm, n), lambda i: (i, 0)),
        out_shape=jax.ShapeDtypeStruct((m, n), F32),
        scratch_shapes=[pltpu.VMEM((tm, n), F32)],
        compiler_params=_params("parallel"),
        name=name,
    )(o, w, x, g, b)


def _gdn_prep_kernel(q_ref, k_ref, v_ref, qp_ref, kp_ref, vp_ref, gates_ref, cw_ref, alog_ref, dtb_ref,
                     wq_ref, u_ref, kdt_ref, attn_ref, egl_ref, xs_ref, p_ref, tm_ref, rhs_ref):
    tb = GDN_BLOCK
    nck = tb // GDN_CHUNK
    first = pl.program_id(1) == 0

    gts = gates_ref[...]
    beta_all = jax.nn.sigmoid(gts)
    xg = gts + dtb_ref[...]
    softplus = jnp.maximum(xg, 0.0) + jnp.log1p(jnp.exp(-jnp.abs(xg)))
    g_all = -jnp.exp(alog_ref[...]) * softplus
    row = lax.broadcasted_iota(jnp.int32, (tb, GATE_LANES), 0)
    rin = jnp.bitwise_and(row, GDN_CHUNK - 1)
    gc = g_all
    s = 1
    while s < GDN_CHUNK:
        gc = gc + jnp.where(rin >= s, pltpu.roll(gc, s, axis=0), 0.0)
        s *= 2
    gc3 = gc.reshape(nck, GDN_CHUNK, GATE_LANES)
    gl = jnp.broadcast_to(gc3[:, GDN_CHUNK - 1:GDN_CHUNK, :], gc3.shape).reshape(tb, GATE_LANES)
    egc = jnp.exp(gc)
    ekd = jnp.exp(gl - gc)
    egl_ref[...] = jnp.exp(gl)
    gct = gc.T

    ri = lax.broadcasted_iota(jnp.int32, (tb, tb), 0)
    ci = lax.broadcasted_iota(jnp.int32, (tb, tb), 1)
    same = jnp.right_shift(ri, 6) == jnp.right_shift(ci, 6)
    causal = same & (ri >= ci)
    strict = same & (ri > ci)
    eye = (ri == ci).astype(F32)
    eye_b = eye.astype(BF16)

    def conv_silu(x_ref, p_ref, slot, h, col0):
        sl = slice(h * HEAD_DIM, (h + 1) * HEAD_DIM)
        xs_ref[slot, 0:8, :] = jnp.where(first, 0.0, p_ref[:, sl])
        xs_ref[slot, 8:8 + tb, :] = x_ref[:, sl]
        cw = cw_ref[:, col0 + h * HEAD_DIM:col0 + (h + 1) * HEAD_DIM]
        acc = xs_ref[slot, 8:8 + tb, :] * cw[CONV_K - 1:CONV_K, :]
        for d in range(1, CONV_K):
            acc = acc + xs_ref[slot, 8 - d:8 - d + tb, :] * cw[CONV_K - 1 - d:CONV_K - d, :]
        return _silu(acc)

    for h in range(HEADS):
        sl = slice(h * HEAD_DIM, (h + 1) * HEAD_DIM)
        qc = conv_silu(q_ref, qp_ref, 3 * h, h, 0)
        kc = conv_silu(k_ref, kp_ref, 3 * h + 1, h, HEADS * HEAD_DIM)
        vc = conv_silu(v_ref, vp_ref, 3 * h + 2, h, 2 * HEADS * HEAD_DIM)
        qn = qc * lax.rsqrt(jnp.sum(qc * qc, axis=-1, keepdims=True) + RMS_EPS)
        kn = kc * lax.rsqrt(jnp.sum(kc * kc, axis=-1, keepdims=True) + RMS_EPS)
        beta = beta_all[:, h:h + 1]
        gcol = gc[:, HEADS + h:HEADS + h + 1]
        egc_c = egc[:, HEADS + h:HEADS + h + 1]
        ekd_c = ekd[:, HEADS + h:HEADS + h + 1]
        grow = gct[HEADS + h:HEADS + h + 1, :]

        qs = qn * (HEAD_DIM ** -0.5)
        kb = kn * beta
        kn_b = kn.astype(BF16)
        kk = lax.dot_general(kb.astype(BF16), kn_b, _NT, preferred_element_type=F32)
        qk = lax.dot_general(qs.astype(BF16), kn_b, _NT, preferred_element_type=F32)
        dm = jnp.exp(jnp.where(causal, gcol - grow, -jnp.inf))
        attn_ref[:, h * tb:(h + 1) * tb] = (qk * dm).astype(BF16)
        nm = jnp.where(strict, -(kk * dm), 0.0)
        p_ref[h] = nm.astype(BF16)
        tm_ref[h] = (eye + nm).astype(BF16)
        rhs_ref[h, :, :HEAD_DIM] = (kb * egc_c).astype(BF16)
        rhs_ref[h, :, HEAD_DIM:] = (vc * beta).astype(BF16)
        qd = (qs * egc_c).astype(BF16)
        for c in range(nck):
            wq_ref[(2 * c + 1) * GDN_CHUNK:(2 * c + 2) * GDN_CHUNK, sl] = qd[c * GDN_CHUNK:(c + 1) * GDN_CHUNK, :]
        kdt_ref[0, sl, :] = (kn * ekd_c).T.astype(BF16)

    for _ in range(GDN_CHUNK.bit_length() - 2):
        for h in range(HEADS):
            p = p_ref[h]
            p_ref[h] = jnp.dot(p, p, preferred_element_type=F32).astype(BF16)
        for h in range(HEADS):
            tm_ref[h] = jnp.dot(tm_ref[h], p_ref[h] + eye_b, preferred_element_type=F32).astype(BF16)
    for h in range(HEADS):
        sl = slice(h * HEAD_DIM, (h + 1) * HEAD_DIM)
        wu = jnp.dot(tm_ref[h], rhs_ref[h], preferred_element_type=F32)
        w = wu[:, :HEAD_DIM].astype(BF16)
        u_ref[:, sl] = wu[:, HEAD_DIM:]
        for c in range(nck):
            wq_ref[2 * c * GDN_CHUNK:(2 * c + 1) * GDN_CHUNK, sl] = w[c * GDN_CHUNK:(c + 1) * GDN_CHUNK, :]


def _gdn_prep(proj, gates, conv_w, alog_row, dtb_row, batch, seq):
    tb = GDN_BLOCK
    nt = seq // tb
    bt = batch * seq
    width = HEADS * HEAD_DIM
    blk = lambda col: pl.BlockSpec((tb, width), lambda b, t: (b * nt + t, col))
    prev = lambda col: pl.BlockSpec((8, width), lambda b, t: (jnp.maximum((b * nt + t) * (tb // 8) - 1, 0), col))
    return pl.pallas_call(
        _gdn_prep_kernel,
        grid=(batch, nt),
        in_specs=[blk(0), blk(1), blk(2), prev(0), prev(1), prev(2),
                  pl.BlockSpec((tb, GATE_LANES), lambda b, t: (b * nt + t, 0)),
                  pl.BlockSpec((CONV_K, 3 * width), lambda b, t: (0, 0)),
                  pl.BlockSpec((1, GATE_LANES), lambda b, t: (0, 0)),
                  pl.BlockSpec((1, GATE_LANES), lambda b, t: (0, 0))],
        out_specs=[pl.BlockSpec((2 * tb, width), lambda b, t: (b * nt + t, 0)),
                   pl.BlockSpec((tb, width), lambda b, t: (b * nt + t, 0)),
                   pl.BlockSpec((1, width, tb), lambda b, t: (b * nt + t, 0, 0)),
                   pl.BlockSpec((tb, HEADS * tb), lambda b, t: (b * nt + t, 0)),
                   pl.BlockSpec((tb, GATE_LANES), lambda b, t: (b * nt + t, 0))],
        out_shape=[jax.ShapeDtypeStruct((2 * bt, width), BF16),
                   jax.ShapeDtypeStruct((bt, width), F32),
                   jax.ShapeDtypeStruct((bt // tb, width, tb), BF16),
                   jax.ShapeDtypeStruct((bt, HEADS * tb), BF16),
                   jax.ShapeDtypeStruct((bt, GATE_LANES), F32)],
        scratch_shapes=[pltpu.VMEM((3 * HEADS, tb + 8, HEAD_DIM), F32),
                        pltpu.VMEM((HEADS, tb, tb), BF16),
                        pltpu.VMEM((HEADS, tb, tb), BF16),
                        pltpu.VMEM((HEADS, tb, 2 * HEAD_DIM), BF16)],
        compiler_params=_params("parallel", "arbitrary"),
        name="gdn_prep",
    )(proj, proj, proj, proj, proj, proj, gates, conv_w, alog_row, dtb_row)


def _gdn_scan_kernel(wq_ref, u_ref, kdt_ref, attn_ref, egl_ref, z_ref, nw_ref, o_ref, s_ref, vn_ref, qs_ref):
    tb = GDN_BLOCK
    nck = tb // GDN_CHUNK
    ck = GDN_CHUNK

    @pl.when(pl.program_id(1) == 0)
    def _():
        s_ref[...] = jnp.zeros_like(s_ref)

    zeros_half = jnp.zeros((ck, HEAD_DIM), BF16)
    for c in range(nck):
        rs = slice(c * ck, (c + 1) * ck)
        for h in range(HEADS):
            sl = slice(h * HEAD_DIM, (h + 1) * HEAD_DIM)
            state = s_ref[h]
            r = jnp.dot(wq_ref[2 * c * ck:(2 * c + 2) * ck, sl], state.astype(BF16),
                        preferred_element_type=F32)
            vnew = (u_ref[rs, sl] - r[:ck, :]).astype(BF16)
            qs_ref[h, rs, :] = r[ck:, :]
            vn_ref[h, rs, :] = vnew
            vpad = jnp.concatenate([vnew, zeros_half] if c % 2 == 0 else [zeros_half, vnew], axis=0)
            slab = kdt_ref[0, sl, (c // 2) * 2 * ck:(c // 2 + 1) * 2 * ck]
            cd = egl_ref[c * ck:c * ck + 1, HEADS + h:HEADS + h + 1]
            s_ref[h] = state * cd + jnp.dot(slab, vpad, preferred_element_type=F32)
    for h in range(HEADS):
        sl = slice(h * HEAD_DIM, (h + 1) * HEAD_DIM)
        o = qs_ref[h] + jnp.dot(attn_ref[:, h * tb:(h + 1) * tb], vn_ref[h], preferred_element_type=F32)
        o = o * lax.rsqrt(jnp.mean(o * o, axis=-1, keepdims=True) + RMS_EPS) * nw_ref[...]
        o_ref[:, sl] = (o * _silu(z_ref[:, sl])).astype(BF16)


def _gdn_scan(wq, u, kdt, attn, egl, proj, norm_w, batch, seq):
    tb = GDN_BLOCK
    nt = seq // tb
    bt = batch * seq
    width = HEADS * HEAD_DIM
    return pl.pallas_call(
        _gdn_scan_kernel,
        grid=(batch, nt),
        in_specs=[pl.BlockSpec((2 * tb, width), lambda b, t: (b * nt + t, 0)),
                  pl.BlockSpec((tb, width), lambda b, t: (b * nt + t, 0)),
                  pl.BlockSpec((1, width, tb), lambda b, t: (b * nt + t, 0, 0)),
                  pl.BlockSpec((tb, HEADS * tb), lambda b, t: (b * nt + t, 0)),
                  pl.BlockSpec((tb, GATE_LANES), lambda b, t: (b * nt + t, 0)),
                  pl.BlockSpec((tb, width), lambda b, t: (b * nt + t, 3)),
                  pl.BlockSpec((1, HEAD_DIM), lambda b, t: (0, 0))],
        out_specs=pl.BlockSpec((tb, width), lambda b, t: (b * nt + t, 0)),
        out_shape=jax.ShapeDtypeStruct((bt, width), BF16),
        scratch_shapes=[pltpu.VMEM((HEADS, HEAD_DIM, HEAD_DIM), F32),
                        pltpu.VMEM((HEADS, tb, HEAD_DIM), BF16),
                        pltpu.VMEM((HEADS, tb, HEAD_DIM), F32)],
        compiler_params=_params("parallel", "arbitrary"),
        name="gdn_scan",
    )(wq, u, kdt, attn, egl, proj, norm_w)


def _gdn_layer(x, w_in, conv_w, a_log, dt_bias, norm_w, w_out, ln_g, ln_b, batch, seq):
    width = HEADS * HEAD_DIM
    w_main = w_in[:, :4 * width].astype(BF16)
    w_gate = jnp.pad(w_in[:, 4 * width:], ((0, 0), (0, GATE_LANES - 2 * HEADS))).astype(BF16)
    pad_row = lambda v: jnp.pad(v.astype(F32), (HEADS, GATE_LANES - 2 * HEADS))[None, :]
    proj, gates = _project(x, [w_main, w_gate], [F32, F32], 512, "gdn_in_proj")
    wq, u, kdt, attn, egl = _gdn_prep(proj, gates, conv_w.astype(F32), pad_row(a_log), pad_row(dt_bias), batch, seq)
    o = _gdn_scan(wq, u, kdt, attn, egl, proj, norm_w.astype(F32)[None, :], batch, seq)
    return _out_proj_ln(o, w_out.astype(BF16), x, ln_g[None, :], ln_b[None, :], 512, "gdn_out_ln")


def _kv_kernel(x_ref, w_ref, kaug_ref, vaug_ref, kmean_ref, *, nq):
    blk = MOBA_BLOCK
    width = HEADS * HEAD_DIM
    aug = 2 * HEAD_DIM
    nblk = x_ref.shape[0] // blk
    lane = lax.broadcasted_iota(jnp.int32, (blk, HEAD_DIM), 1)
    ones_col = (lane == 0).astype(BF16)
    for r in range(nblk):
        rs = slice(r * blk, (r + 1) * blk)
        acc = jnp.dot(x_ref[rs, :].astype(BF16), w_ref[...], preferred_element_type=F32)
        onehot = (lane == lax.rem(pl.program_id(0) * nblk + r, nq)).astype(BF16)
        for h in range(HEADS):
            kaug_ref[rs, h * aug:h * aug + HEAD_DIM] = acc[:, h * HEAD_DIM:(h + 1) * HEAD_DIM].astype(BF16)
            kaug_ref[rs, h * aug + HEAD_DIM:(h + 1) * aug] = onehot
            vaug_ref[rs, h * aug:h * aug + HEAD_DIM] = acc[:, width + h * HEAD_DIM:width + (h + 1) * HEAD_DIM].astype(BF16)
            vaug_ref[rs, h * aug + HEAD_DIM:(h + 1) * aug] = ones_col
        row = lax.rem(pl.program_id(0), 8 // nblk) * nblk + r
        kmean_ref[pl.ds(row, 1), :] = jnp.mean(acc[:, :width], axis=0, keepdims=True)


def _shared_kv(x, w_kv, seq):
    m, k = x.shape
    n = w_kv.shape[1]
    tm = 4 * MOBA_BLOCK
    steps_per_mean_block = 8 * MOBA_BLOCK // tm
    return pl.pallas_call(
        functools.partial(_kv_kernel, nq=seq // MOBA_BLOCK),
        grid=(m // tm,),
        in_specs=[pl.BlockSpec((tm, k), lambda i: (i, 0)),
                  pl.BlockSpec((k, n), lambda i: (0, 0))],
        out_specs=[pl.BlockSpec((tm, n), lambda i: (i, 0)),
                   pl.BlockSpec((tm, n), lambda i: (i, 0)),
                   pl.BlockSpec((8, n // 2), lambda i: (i // steps_per_mean_block, 0))],
        out_shape=[jax.ShapeDtypeStruct((m, n), BF16),
                   jax.ShapeDtypeStruct((m, n), BF16),
                   jax.ShapeDtypeStruct((m // MOBA_BLOCK, n // 2), F32)],
        compiler_params=_params("arbitrary"),
        name="moba_shared_kv",
    )(x, w_kv)


def _moba_select_bias(q_b, kmean, qb):
    nb = kmean.shape[0]
    blk = q_b.shape[0]
    km = jnp.concatenate([kmean, jnp.zeros((HEAD_DIM - nb, HEAD_DIM), F32)], axis=0).astype(BF16)
    gate_t = lax.dot_general(km, q_b, _NT, preferred_element_type=F32)[:nb]
    jrow = lax.broadcasted_iota(jnp.int32, (nb, blk), 0)
    valid = jrow < qb
    jf = jrow.astype(F32)
    g = jnp.where(valid, gate_t, -jnp.inf)
    keep = jnp.zeros((nb, blk), F32)
    for _ in range(MOBA_TOP_K):
        mx = jnp.max(g, axis=0, keepdims=True)
        first = jnp.min(jnp.where(g == mx, jf, float(nb)), axis=0, keepdims=True)
        sel = jf == first
        keep = jnp.where(sel, 1.0, keep)
        g = jnp.where(sel, -jnp.inf, g)
    bias_t = jnp.where(valid & (keep > 0.0), 0.0, MASK_VALUE)
    bias_t = jnp.concatenate([bias_t, jnp.zeros((HEAD_DIM - nb, blk), F32)], axis=0)
    return bias_t.T


def _moba_attn_kernel(q_ref, z_ref, k_ref, v_ref, kmean_ref, o_ref, acc_ref):
    blk = MOBA_BLOCK
    grp = MOBA_GROUP
    aug = 2 * HEAD_DIM
    qb = pl.program_id(2)
    own = pl.multiple_of(qb * blk, blk)
    qi = lax.broadcasted_iota(jnp.int32, (blk, blk), 0)
    ki = lax.broadcasted_iota(jnp.int32, (blk, blk), 1)
    causal = ki <= qi
    zeros_b = jnp.zeros((blk, HEAD_DIM), BF16)

    heads = range(MOBA_HEADS_PER_STEP)
    ha = [slice(h * aug, (h + 1) * aug) for h in heads]
    q_bs = [(q_ref[:, h * HEAD_DIM:(h + 1) * HEAD_DIM] * (HEAD_DIM ** -0.5)).astype(BF16) for h in heads]
    q_augs = [jnp.concatenate([q_bs[h], _moba_select_bias(
        q_bs[h], kmean_ref[:, h * HEAD_DIM:(h + 1) * HEAD_DIM], qb).astype(BF16)], axis=1) for h in heads]
    s_own = [jnp.where(causal, lax.dot_general(jnp.concatenate([q_bs[h], zeros_b], axis=1),
                                               k_ref[pl.ds(own, blk), ha[h]], _NT, preferred_element_type=F32),
                       MASK_VALUE) for h in heads]
    s_rest = [lax.dot_general(q_augs[h], k_ref[0:(grp - 1) * blk, ha[h]], _NT, preferred_element_type=F32)
              for h in heads]
    m_init = [jnp.maximum(jnp.max(s_own[h], axis=-1, keepdims=True), jnp.max(s_rest[h], axis=-1, keepdims=True))
              for h in heads]
    for h in heads:
        acc_ref[h] = (jnp.dot(jnp.exp(s_own[h] - m_init[h]).astype(BF16), v_ref[pl.ds(own, blk), ha[h]],
                              preferred_element_type=F32)
                      + jnp.dot(jnp.exp(s_rest[h] - m_init[h]).astype(BF16), v_ref[0:(grp - 1) * blk, ha[h]],
                                preferred_element_type=F32))

    def group_step(start, ms):
        s = [lax.dot_general(q_augs[h], k_ref[pl.ds(start, grp * blk), ha[h]], _NT, preferred_element_type=F32)
             for h in heads]
        m_new = [jnp.maximum(ms[h], jnp.max(s[h], axis=-1, keepdims=True)) for h in heads]
        p = [jnp.exp(s[h] - m_new[h]).astype(BF16) for h in heads]
        for h in heads:
            acc_ref[h] = jnp.exp(ms[h] - m_new[h]) * acc_ref[h] + jnp.dot(
                p[h], v_ref[pl.ds(start, grp * blk), ha[h]], preferred_element_type=F32)
        return tuple(m_new)

    def group_start(g):
        return pl.multiple_of((grp - 1 + g * grp) * blk, blk)

    n_groups = (jnp.maximum(qb - (grp - 1), 0) + grp - 1) // grp
    ms = lax.fori_loop(0, n_groups // 2,
                       lambda i, ms: group_step(group_start(2 * i + 1), group_step(group_start(2 * i), ms)),
                       tuple(m_init))

    @pl.when(lax.rem(n_groups, 2) == 1)
    def _():
        group_step(group_start(n_groups - 1), ms)

    for h in heads:
        hs = slice(h * HEAD_DIM, (h + 1) * HEAD_DIM)
        acc = acc_ref[h]
        o = acc[:, :HEAD_DIM] / acc[:, HEAD_DIM:HEAD_DIM + 1]
        o_ref[:, hs] = (o * _silu(z_ref[:, hs])).astype(BF16)


def _moba_attn(qz, kaug, vaug, kmean, batch, seq):
    blk = MOBA_BLOCK
    nq = seq // blk
    bt = batch * seq
    hb = MOBA_HEADS_PER_STEP
    groups = HEADS // hb
    assert nq % MOBA_GROUP == 0
    return pl.pallas_call(
        _moba_attn_kernel,
        grid=(batch, groups, nq),
        in_specs=[pl.BlockSpec((blk, hb * HEAD_DIM), lambda b, h, i: (b * nq + i, h)),
                  pl.BlockSpec((blk, hb * HEAD_DIM), lambda b, h, i: (b * nq + i, groups + h)),
                  pl.BlockSpec((seq, hb * 2 * HEAD_DIM), lambda b, h, i: (b, h), pipeline_mode=pl.Buffered(1)),
                  pl.BlockSpec((seq, hb * 2 * HEAD_DIM), lambda b, h, i: (b, h), pipeline_mode=pl.Buffered(1)),
                  pl.BlockSpec((nq, hb * HEAD_DIM), lambda b, h, i: (b, h))],
        out_specs=pl.BlockSpec((blk, hb * HEAD_DIM), lambda b, h, i: (b * nq + i, h)),
        out_shape=jax.ShapeDtypeStruct((bt, HEADS * HEAD_DIM), BF16),
        scratch_shapes=[pltpu.VMEM((hb, blk, 2 * HEAD_DIM), F32)],
        compiler_params=_params("parallel", "parallel", "arbitrary"),
        name="moba_attn",
    )(qz, qz, kaug, vaug, kmean)


def _moba_layer(x, w_in, w_out, ln_g, ln_b, kaug, vaug, kmean, batch, seq):
    (qz,) = _project(x, [w_in.astype(BF16)], [F32], 1024, "moba_in_proj")
    o = _moba_attn(qz, kaug, vaug, kmean, batch, seq)
    return _out_proj_ln(o, w_out.astype(BF16), x, ln_g[None, :], ln_b[None, :], 512, "moba_out_ln")


def kernel(x, a_w_in, a_conv_w, a_A_log, a_dt_bias, a_norm_w, a_w_out, a_ln_g, a_ln_b, b_w_kv, b_w_in, b_w_out, b_ln_g, b_ln_b):
    batch, seq, d_model = x.shape
    assert d_model == D_MODEL and seq % (MOBA_GROUP * MOBA_BLOCK) == 0 and seq // MOBA_BLOCK <= HEAD_DIM
    assert a_w_in.shape[2] == 4 * HEADS * HEAD_DIM + 2 * HEADS and a_conv_w.shape[1] == CONV_K
    h = x.reshape(batch * seq, d_model)
    for i in range(a_w_in.shape[0]):
        h = _gdn_layer(h, a_w_in[i], a_conv_w[i], a_A_log[i], a_dt_bias[i], a_norm_w[i], a_w_out[i],
                       a_ln_g[i], a_ln_b[i], batch, seq)
    kaug, vaug, kmean = _shared_kv(h, b_w_kv.astype(BF16), seq)
    for i in range(b_w_in.shape[0]):
        h = _moba_layer(h, b_w_in[i], b_w_out[i], b_ln_g[i], b_ln_b[i], kaug, vaug, kmean, batch, seq)
    return h.reshape(batch, seq, d_model)
```

```python
import functools

import jax
import jax.numpy as jnp
from jax import lax
from jax.experimental import pallas as pl
from jax.experimental.pallas import tpu as pltpu

F32 = jnp.float32
BF16 = jnp.bfloat16

D_MODEL = 1024
HEADS = 8
HEAD_DIM = 128
CONV_K = 4
GDN_CHUNK = 64
GDN_BLOCK = 256
MOBA_BLOCK = 256
MOBA_TOP_K = 3
MOBA_GROUP = 4
MOBA_HEADS_PER_STEP = 4
DEPTH = 4
DEEPNORM_ALPHA = (2 * DEPTH) ** 0.25
LN_EPS = 1e-5
RMS_EPS = 1e-6
MASK_VALUE = -1e30
GATE_LANES = 128
V7X_VMEM_LIMIT = 56 * 1024 * 1024
PROJ_COL_TILE = 1024
OUT_LN_PARTS = 4

_NT = (((1,), (1,)), ((), ()))


def _params(*sem):
    return pltpu.CompilerParams(dimension_semantics=sem, vmem_limit_bytes=V7X_VMEM_LIMIT)


def _silu(x):
    return x * jax.nn.sigmoid(x)


def _proj_kernel(x_ref, *refs):
    nw = len(refs) // 2
    xb = x_ref[...].astype(BF16)
    for w_ref, o_ref in zip(refs[:nw], refs[nw:]):
        n = w_ref.shape[1]
        tn = min(n, PROJ_COL_TILE)
        for c in range(n // tn):
            o_ref[:, c * tn:(c + 1) * tn] = jnp.dot(xb, w_ref[:, c * tn:(c + 1) * tn],
                                                    preferred_element_type=F32).astype(o_ref.dtype)


def _project(x, weights, out_dtypes, tm, name):
    m, k = x.shape
    return pl.pallas_call(
        _proj_kernel,
        grid=(m // tm,),
        in_specs=[pl.BlockSpec((tm, k), lambda i: (i, 0))]
        + [pl.BlockSpec(w.shape, lambda i: (0, 0)) for w in weights],
        out_specs=[pl.BlockSpec((tm, w.shape[1]), lambda i: (i, 0)) for w in weights],
        out_shape=[jax.ShapeDtypeStruct((m, w.shape[1]), dt) for w, dt in zip(weights, out_dtypes)],
        compiler_params=_params("parallel"),
        name=name,
    )(x, *weights)


def _deepnorm_rows(x_ref, y_ref, g_ref, b_ref, out_ref, start, stop):
    rows = 64
    for r in range(start // rows, stop // rows):
        rs = slice(r * rows, (r + 1) * rows)
        s = DEEPNORM_ALPHA * x_ref[rs, :] + y_ref[rs, :]
        mu = jnp.mean(s, axis=-1, keepdims=True)
        c = s - mu
        var = jnp.mean(c * c, axis=-1, keepdims=True)
        out_ref[rs, :] = c * lax.rsqrt(var + LN_EPS) * g_ref[...] + b_ref[...]


def _out_ln_kernel(o_ref, w_ref, x_ref, g_ref, b_ref, out_ref, y_ref):
    parts = OUT_LN_PARTS
    size = o_ref.shape[0] // parts

    def matmul_part(i):
        y_ref[i * size:(i + 1) * size, :] = jnp.dot(o_ref[i * size:(i + 1) * size, :], w_ref[...],
                                                    preferred_element_type=F32)

    matmul_part(0)
    for i in range(parts):
        if i + 1 < parts:
            matmul_part(i + 1)
        _deepnorm_rows(x_ref, y_ref, g_ref, b_ref, out_ref, i * size, (i + 1) * size)


def _out_proj_ln(o, w, x, g, b, tm, name):
    m, k = o.shape
    n = w.shape[1]
    return pl.pallas_call(
        _out_ln_kernel,
        grid=(m // tm,),
        in_specs=[pl.BlockSpec((tm, k), lambda i: (i, 0)),
                  pl.BlockSpec((k, n), lambda i: (0, 0)),
                  pl.BlockSpec((tm, n), lambda i: (i, 0)),
                  pl.BlockSpec((1, n), lambda i: (0, 0)),
                  pl.BlockSpec((1, n), lambda i: (0, 0))],
        out_specs=pl.BlockSpec((tm, n), lambda i: (i, 0)),
        out_shape=jax.ShapeDtypeStruct((m, n), F32),
        scratch_shapes=[pltpu.VMEM((tm, n), F32)],
        compiler_params=_params("parallel"),
        name=name,
    )(o, w, x, g, b)


def _gdn_prep_kernel(q_ref, k_ref, v_ref, qp_ref, kp_ref, vp_ref, gates_ref, cw_ref, alog_ref, dtb_ref,
                     wq_ref, u_ref, kdt_ref, attn_ref, egl_ref, xs_ref, p_ref, tm_ref, rhs_ref):
    tb = GDN_BLOCK
    nck = tb // GDN_CHUNK
    first = pl.program_id(1) == 0

    gts = gates_ref[...]
    beta_all = jax.nn.sigmoid(gts)
    xg = gts + dtb_ref[...]
    softplus = jnp.maximum(xg, 0.0) + jnp.log1p(jnp.exp(-jnp.abs(xg)))
    g_all = -jnp.exp(alog_ref[...]) * softplus
    row = lax.broadcasted_iota(jnp.int32, (tb, GATE_LANES), 0)
    rin = jnp.bitwise_and(row, GDN_CHUNK - 1)
    gc = g_all
    s = 1
    while s < GDN_CHUNK:
        gc = gc + jnp.where(rin >= s, pltpu.roll(gc, s, axis=0), 0.0)
        s *= 2
    gc3 = gc.reshape(nck, GDN_CHUNK, GATE_LANES)
    gl = jnp.broadcast_to(gc3[:, GDN_CHUNK - 1:GDN_CHUNK, :], gc3.shape).reshape(tb, GATE_LANES)
    egc = jnp.exp(gc)
    ekd = jnp.exp(gl - gc)
    egl_ref[...] = jnp.exp(gl)
    gct = gc.T

    ri = lax.broadcasted_iota(jnp.int32, (tb, tb), 0)
    ci = lax.broadcasted_iota(jnp.int32, (tb, tb), 1)
    same = jnp.right_shift(ri, 6) == jnp.right_shift(ci, 6)
    causal = same & (ri >= ci)
    strict = same & (ri > ci)
    eye = (ri == ci).astype(F32)
    eye_b = eye.astype(BF16)

    def conv_silu(x_ref, p_ref, slot, h, col0):
        sl = slice(h * HEAD_DIM, (h + 1) * HEAD_DIM)
        xs_ref[slot, 0:8, :] = jnp.where(first, 0.0, p_ref[:, sl])
        xs_ref[slot, 8:8 + tb, :] = x_ref[:, sl]
        cw = cw_ref[:, col0 + h * HEAD_DIM:col0 + (h + 1) * HEAD_DIM]
        acc = xs_ref[slot, 8:8 + tb, :] * cw[CONV_K - 1:CONV_K, :]
        for d in range(1, CONV_K):
            acc = acc + xs_ref[slot, 8 - d:8 - d + tb, :] * cw[CONV_K - 1 - d:CONV_K - d, :]
        return _silu(acc)

    for h in range(HEADS):
        sl = slice(h * HEAD_DIM, (h + 1) * HEAD_DIM)
        qc = conv_silu(q_ref, qp_ref, 3 * h, h, 0)
        kc = conv_silu(k_ref, kp_ref, 3 * h + 1, h, HEADS * HEAD_DIM)
        vc = conv_silu(v_ref, vp_ref, 3 * h + 2, h, 2 * HEADS * HEAD_DIM)
        qn = qc * lax.rsqrt(jnp.sum(qc * qc, axis=-1, keepdims=True) + RMS_EPS)
        kn = kc * lax.rsqrt(jnp.sum(kc * kc, axis=-1, keepdims=True) + RMS_EPS)
        beta = beta_all[:, h:h + 1]
        gcol = gc[:, HEADS + h:HEADS + h + 1]
        egc_c = egc[:, HEADS + h:HEADS + h + 1]
        ekd_c = ekd[:, HEADS + h:HEADS + h + 1]
        grow = gct[HEADS + h:HEADS + h + 1, :]

        qs = qn * (HEAD_DIM ** -0.5)
        kb = kn * beta
        kn_b = kn.astype(BF16)
        kk = lax.dot_general(kb.astype(BF16), kn_b, _NT, preferred_element_type=F32)
        qk = lax.dot_general(qs.astype(BF16), kn_b, _NT, preferred_element_type=F32)
        dm = jnp.exp(jnp.where(causal, gcol - grow, -jnp.inf))
        attn_ref[:, h * tb:(h + 1) * tb] = (qk * dm).astype(BF16)
        nm = jnp.where(strict, -(kk * dm), 0.0)
        p_ref[h] = nm.astype(BF16)
        tm_ref[h] = (eye + nm).astype(BF16)
        rhs_ref[h, :, :HEAD_DIM] = (kb * egc_c).astype(BF16)
        rhs_ref[h, :, HEAD_DIM:] = (vc * beta).astype(BF16)
        qd = (qs * egc_c).astype(BF16)
        for c in range(nck):
            wq_ref[(2 * c + 1) * GDN_CHUNK:(2 * c + 2) * GDN_CHUNK, sl] = qd[c * GDN_CHUNK:(c + 1) * GDN_CHUNK, :]
        kdt_ref[0, sl, :] = (kn * ekd_c).T.astype(BF16)

    for _ in range(GDN_CHUNK.bit_length() - 2):
        for h in range(HEADS):
            p = p_ref[h]
            p_ref[h] = jnp.dot(p, p, preferred_element_type=F32).astype(BF16)
        for h in range(HEADS):
            tm_ref[h] = jnp.dot(tm_ref[h], p_ref[h] + eye_b, preferred_element_type=F32).astype(BF16)
    for h in range(HEADS):
        sl = slice(h * HEAD_DIM, (h + 1) * HEAD_DIM)
        wu = jnp.dot(tm_ref[h], rhs_ref[h], preferred_element_type=F32)
        w = wu[:, :HEAD_DIM].astype(BF16)
        u_ref[:, sl] = wu[:, HEAD_DIM:]
        for c in range(nck):
            wq_ref[2 * c * GDN_CHUNK:(2 * c + 1) * GDN_CHUNK, sl] = w[c * GDN_CHUNK:(c + 1) * GDN_CHUNK, :]


def _gdn_prep(proj, gates, conv_w, alog_row, dtb_row, batch, seq):
    tb = GDN_BLOCK
    nt = seq // tb
    bt = batch * seq
    width = HEADS * HEAD_DIM
    blk = lambda col: pl.BlockSpec((tb, width), lambda b, t: (b * nt + t, col))
    prev = lambda col: pl.BlockSpec((8, width), lambda b, t: (jnp.maximum((b * nt + t) * (tb // 8) - 1, 0), col))
    return pl.pallas_call(
        _gdn_prep_kernel,
        grid=(batch, nt),
        in_specs=[blk(0), blk(1), blk(2), prev(0), prev(1), prev(2),
                  pl.BlockSpec((tb, GATE_LANES), lambda b, t: (b * nt + t, 0)),
                  pl.BlockSpec((CONV_K, 3 * width), lambda b, t: (0, 0)),
                  pl.BlockSpec((1, GATE_LANES), lambda b, t: (0, 0)),
                  pl.BlockSpec((1, GATE_LANES), lambda b, t: (0, 0))],
        out_specs=[pl.BlockSpec((2 * tb, width), lambda b, t: (b * nt + t, 0)),
                   pl.BlockSpec((tb, width), lambda b, t: (b * nt + t, 0)),
                   pl.BlockSpec((1, width, tb), lambda b, t: (b * nt + t, 0, 0)),
                   pl.BlockSpec((tb, HEADS * tb), lambda b, t: (b * nt + t, 0)),
                   pl.BlockSpec((tb, GATE_LANES), lambda b, t: (b * nt + t, 0))],
        out_shape=[jax.ShapeDtypeStruct((2 * bt, width), BF16),
                   jax.ShapeDtypeStruct((bt, width), F32),
                   jax.ShapeDtypeStruct((bt // tb, width, tb), BF16),
                   jax.ShapeDtypeStruct((bt, HEADS * tb), BF16),
                   jax.ShapeDtypeStruct((bt, GATE_LANES), F32)],
        scratch_shapes=[pltpu.VMEM((3 * HEADS, tb + 8, HEAD_DIM), F32),
                        pltpu.VMEM((HEADS, tb, tb), BF16),
                        pltpu.VMEM((HEADS, tb, tb), BF16),
                        pltpu.VMEM((HEADS, tb, 2 * HEAD_DIM), BF16)],
        compiler_params=_params("parallel", "arbitrary"),
        name="gdn_prep",
    )(proj, proj, proj, proj, proj, proj, gates, conv_w, alog_row, dtb_row)


def _gdn_scan_kernel(wq_ref, u_ref, kdt_ref, attn_ref, egl_ref, z_ref, nw_ref, wo_ref, x_ref, g_ref, b_ref,
                     out_ref, s_ref, vn_ref, qs_ref, o_ref, y_ref):
    tb = GDN_BLOCK
    nck = tb // GDN_CHUNK
    ck = GDN_CHUNK

    @pl.when(pl.program_id(1) == 0)
    def _():
        s_ref[...] = jnp.zeros_like(s_ref)

    zeros_half = jnp.zeros((ck, HEAD_DIM), BF16)
    for c in range(nck):
        rs = slice(c * ck, (c + 1) * ck)
        for h in range(HEADS):
            sl = slice(h * HEAD_DIM, (h + 1) * HEAD_DIM)
            state = s_ref[h]
            r = jnp.dot(wq_ref[2 * c * ck:(2 * c + 2) * ck, sl], state.astype(BF16),
                        preferred_element_type=F32)
            vnew = (u_ref[rs, sl] - r[:ck, :]).astype(BF16)
            qs_ref[h, rs, :] = r[ck:, :]
            vn_ref[h, rs, :] = vnew
            vpad = jnp.concatenate([vnew, zeros_half] if c % 2 == 0 else [zeros_half, vnew], axis=0)
            slab = kdt_ref[0, sl, (c // 2) * 2 * ck:(c // 2 + 1) * 2 * ck]
            cd = egl_ref[c * ck:c * ck + 1, HEADS + h:HEADS + h + 1]
            s_ref[h] = state * cd + jnp.dot(slab, vpad, preferred_element_type=F32)
    for h in range(HEADS):
        sl = slice(h * HEAD_DIM, (h + 1) * HEAD_DIM)
        o = qs_ref[h] + jnp.dot(attn_ref[:, h * tb:(h + 1) * tb], vn_ref[h], preferred_element_type=F32)
        o = o * lax.rsqrt(jnp.mean(o * o, axis=-1, keepdims=True) + RMS_EPS) * nw_ref[...]
        o_ref[:, sl] = (o * _silu(z_ref[:, sl])).astype(BF16)
    y_ref[...] = jnp.dot(o_ref[...], wo_ref[...], preferred_element_type=F32)
    _deepnorm_rows(x_ref, y_ref, g_ref, b_ref, out_ref, 0, tb)


def _gdn_scan(wq, u, kdt, attn, egl, proj, norm_w, w_out, x, ln_g, ln_b, batch, seq):
    tb = GDN_BLOCK
    nt = seq // tb
    bt = batch * seq
    width = HEADS * HEAD_DIM
    return pl.pallas_call(
        _gdn_scan_kernel,
        grid=(batch, nt),
        in_specs=[pl.BlockSpec((2 * tb, width), lambda b, t: (b * nt + t, 0)),
                  pl.BlockSpec((tb, width), lambda b, t: (b * nt + t, 0)),
                  pl.BlockSpec((1, width, tb), lambda b, t: (b * nt + t, 0, 0)),
                  pl.BlockSpec((tb, HEADS * tb), lambda b, t: (b * nt + t, 0)),
                  pl.BlockSpec((tb, GATE_LANES), lambda b, t: (b * nt + t, 0)),
                  pl.BlockSpec((tb, width), lambda b, t: (b * nt + t, 3)),
                  pl.BlockSpec((1, HEAD_DIM), lambda b, t: (0, 0)),
                  pl.BlockSpec((width, D_MODEL), lambda b, t: (0, 0)),
                  pl.BlockSpec((tb, D_MODEL), lambda b, t: (b * nt + t, 0)),
                  pl.BlockSpec((1, D_MODEL), lambda b, t: (0, 0)),
                  pl.BlockSpec((1, D_MODEL), lambda b, t: (0, 0))],
        out_specs=pl.BlockSpec((tb, D_MODEL), lambda b, t: (b * nt + t, 0)),
        out_shape=jax.ShapeDtypeStruct((bt, D_MODEL), F32),
        scratch_shapes=[pltpu.VMEM((HEADS, HEAD_DIM, HEAD_DIM), F32),
                        pltpu.VMEM((HEADS, tb, HEAD_DIM), BF16),
                        pltpu.VMEM((HEADS, tb, HEAD_DIM), F32),
                        pltpu.VMEM((tb, width), BF16),
                        pltpu.VMEM((tb, D_MODEL), F32)],
        compiler_params=_params("parallel", "arbitrary"),
        name="gdn_scan",
    )(wq, u, kdt, attn, egl, proj, norm_w, w_out, x, ln_g, ln_b)


def _gdn_layer(x, w_in, conv_w, a_log, dt_bias, norm_w, w_out, ln_g, ln_b, batch, seq):
    width = HEADS * HEAD_DIM
    w_main = w_in[:, :4 * width].astype(BF16)
    w_gate = jnp.pad(w_in[:, 4 * width:], ((0, 0), (0, GATE_LANES - 2 * HEADS))).astype(BF16)
    pad_row = lambda v: jnp.pad(v.astype(F32), (HEADS, GATE_LANES - 2 * HEADS))[None, :]
    proj, gates = _project(x, [w_main, w_gate], [F32, F32], 512, "gdn_in_proj")
    wq, u, kdt, attn, egl = _gdn_prep(proj, gates, conv_w.astype(F32), pad_row(a_log), pad_row(dt_bias), batch, seq)
    return _gdn_scan(wq, u, kdt, attn, egl, proj, norm_w.astype(F32)[None, :],
                     w_out.astype(BF16), x, ln_g[None, :], ln_b[None, :], batch, seq)


def _kv_kernel(x_ref, w_ref, kaug_ref, vaug_ref, kmean_ref, *, nq):
    blk = MOBA_BLOCK
    width = HEADS * HEAD_DIM
    aug = 2 * HEAD_DIM
    nblk = x_ref.shape[0] // blk
    lane = lax.broadcasted_iota(jnp.int32, (blk, HEAD_DIM), 1)
    ones_col = (lane == 0).astype(BF16)
    for r in range(nblk):
        rs = slice(r * blk, (r + 1) * blk)
        acc = jnp.dot(x_ref[rs, :].astype(BF16), w_ref[...], preferred_element_type=F32)
        onehot = (lane == lax.rem(pl.program_id(0) * nblk + r, nq)).astype(BF16)
        for h in range(HEADS):
            kaug_ref[rs, h * aug:h * aug + HEAD_DIM] = acc[:, h * HEAD_DIM:(h + 1) * HEAD_DIM].astype(BF16)
            kaug_ref[rs, h * aug + HEAD_DIM:(h + 1) * aug] = onehot
            vaug_ref[rs, h * aug:h * aug + HEAD_DIM] = acc[:, width + h * HEAD_DIM:width + (h + 1) * HEAD_DIM].astype(BF16)
            vaug_ref[rs, h * aug + HEAD_DIM:(h + 1) * aug] = ones_col
        row = lax.rem(pl.program_id(0), 8 // nblk) * nblk + r
        kmean_ref[pl.ds(row, 1), :] = jnp.mean(acc[:, :width], axis=0, keepdims=True)


def _shared_kv(x, w_kv, seq):
    m, k = x.shape
    n = w_kv.shape[1]
    tm = 4 * MOBA_BLOCK
    steps_per_mean_block = 8 * MOBA_BLOCK // tm
    return pl.pallas_call(
        functools.partial(_kv_kernel, nq=seq // MOBA_BLOCK),
        grid=(m // tm,),
        in_specs=[pl.BlockSpec((tm, k), lambda i: (i, 0)),
                  pl.BlockSpec((k, n), lambda i: (0, 0))],
        out_specs=[pl.BlockSpec((tm, n), lambda i: (i, 0)),
                   pl.BlockSpec((tm, n), lambda i: (i, 0)),
                   pl.BlockSpec((8, n // 2), lambda i: (i // steps_per_mean_block, 0))],
        out_shape=[jax.ShapeDtypeStruct((m, n), BF16),
                   jax.ShapeDtypeStruct((m, n), BF16),
                   jax.ShapeDtypeStruct((m // MOBA_BLOCK, n // 2), F32)],
        compiler_params=_params("arbitrary"),
        name="moba_shared_kv",
    )(x, w_kv)


def _moba_select_bias(q_b, kmean, qb):
    nb = kmean.shape[0]
    blk = q_b.shape[0]
    km = jnp.concatenate([kmean, jnp.zeros((HEAD_DIM - nb, HEAD_DIM), F32)], axis=0).astype(BF16)
    gate_t = lax.dot_general(km, q_b, _NT, preferred_element_type=F32)[:nb]
    jrow = lax.broadcasted_iota(jnp.int32, (nb, blk), 0)
    valid = jrow < qb
    jf = jrow.astype(F32)
    g = jnp.where(valid, gate_t, -jnp.inf)
    keep = jnp.zeros((nb, blk), F32)
    for _ in range(MOBA_TOP_K):
        mx = jnp.max(g, axis=0, keepdims=True)
        first = jnp.min(jnp.where(g == mx, jf, float(nb)), axis=0, keepdims=True)
        sel = jf == first
        keep = jnp.where(sel, 1.0, keep)
        g = jnp.where(sel, -jnp.inf, g)
    bias_t = jnp.where(valid & (keep > 0.0), 0.0, MASK_VALUE)
    bias_t = jnp.concatenate([bias_t, jnp.zeros((HEAD_DIM - nb, blk), F32)], axis=0)
    return bias_t.T


def _moba_attn_kernel(q_ref, z_ref, k_ref, v_ref, kmean_ref, o_ref, acc_ref):
    blk = MOBA_BLOCK
    grp = MOBA_GROUP
    aug = 2 * HEAD_DIM
    qb = pl.program_id(2)
    own = pl.multiple_of(qb * blk, blk)
    qi = lax.broadcasted_iota(jnp.int32, (blk, blk), 0)
    ki = lax.broadcasted_iota(jnp.int32, (blk, blk), 1)
    causal = ki <= qi
    zeros_b = jnp.zeros((blk, HEAD_DIM), BF16)

    heads = range(MOBA_HEADS_PER_STEP)
    ha = [slice(h * aug, (h + 1) * aug) for h in heads]
    q_bs = [(q_ref[:, h * HEAD_DIM:(h + 1) * HEAD_DIM] * (HEAD_DIM ** -0.5)).astype(BF16) for h in heads]
    q_augs = [jnp.concatenate([q_bs[h], _moba_select_bias(
        q_bs[h], kmean_ref[:, h * HEAD_DIM:(h + 1) * HEAD_DIM], qb).astype(BF16)], axis=1) for h in heads]
    s_own = [jnp.where(causal, lax.dot_general(jnp.concatenate([q_bs[h], zeros_b], axis=1),
                                               k_ref[pl.ds(own, blk), ha[h]], _NT, preferred_element_type=F32),
                       MASK_VALUE) for h in heads]
    s_rest = [lax.dot_general(q_augs[h], k_ref[0:(grp - 1) * blk, ha[h]], _NT, preferred_element_type=F32)
              for h in heads]
    m_init = [jnp.maximum(jnp.max(s_own[h], axis=-1, keepdims=True), jnp.max(s_rest[h], axis=-1, keepdims=True))
              for h in heads]
    for h in heads:
        acc_ref[h] = (jnp.dot(jnp.exp(s_own[h] - m_init[h]).astype(BF16), v_ref[pl.ds(own, blk), ha[h]],
                              preferred_element_type=F32)
                      + jnp.dot(jnp.exp(s_rest[h] - m_init[h]).astype(BF16), v_ref[0:(grp - 1) * blk, ha[h]],
                                preferred_element_type=F32))

    def group_step(start, ms):
        s = [lax.dot_general(q_augs[h], k_ref[pl.ds(start, grp * blk), ha[h]], _NT, preferred_element_type=F32)
             for h in heads]
        m_new = [jnp.maximum(ms[h], jnp.max(s[h], axis=-1, keepdims=True)) for h in heads]
        p = [jnp.exp(s[h] - m_new[h]).astype(BF16) for h in heads]
        for h in heads:
            acc_ref[h] = jnp.exp(ms[h] - m_new[h]) * acc_ref[h] + jnp.dot(
                p[h], v_ref[pl.ds(start, grp * blk), ha[h]], preferred_element_type=F32)
        return tuple(m_new)

    def group_start(g):
        return pl.multiple_of((grp - 1 + g * grp) * blk, blk)

    n_groups = (jnp.maximum(qb - (grp - 1), 0) + grp - 1) // grp
    ms = lax.fori_loop(0, n_groups // 2,
                       lambda i, ms: group_step(group_start(2 * i + 1), group_step(group_start(2 * i), ms)),
                       tuple(m_init))

    @pl.when(lax.rem(n_groups, 2) == 1)
    def _():
        group_step(group_start(n_groups - 1), ms)

    for h in heads:
        hs = slice(h * HEAD_DIM, (h + 1) * HEAD_DIM)
        acc = acc_ref[h]
        o = acc[:, :HEAD_DIM] / acc[:, HEAD_DIM:HEAD_DIM + 1]
        o_ref[:, hs] = (o * _silu(z_ref[:, hs])).astype(BF16)


def _moba_attn(qz, kaug, vaug, kmean, batch, seq):
    blk = MOBA_BLOCK
    nq = seq // blk
    bt = batch * seq
    hb = MOBA_HEADS_PER_STEP
    groups = HEADS // hb
    assert nq % MOBA_GROUP == 0
    return pl.pallas_call(
        _moba_attn_kernel,
        grid=(batch, groups, nq),
        in_specs=[pl.BlockSpec((blk, hb * HEAD_DIM), lambda b, h, i: (b * nq + i, h)),
                  pl.BlockSpec((blk, hb * HEAD_DIM), lambda b, h, i: (b * nq + i, groups + h)),
                  pl.BlockSpec((seq, hb * 2 * HEAD_DIM), lambda b, h, i: (b, h), pipeline_mode=pl.Buffered(1)),
                  pl.BlockSpec((seq, hb * 2 * HEAD_DIM), lambda b, h, i: (b, h), pipeline_mode=pl.Buffered(1)),
                  pl.BlockSpec((nq, hb * HEAD_DIM), lambda b, h, i: (b, h))],
        out_specs=pl.BlockSpec((blk, hb * HEAD_DIM), lambda b, h, i: (b * nq + i, h)),
        out_shape=jax.ShapeDtypeStruct((bt, HEADS * HEAD_DIM), BF16),
        scratch_shapes=[pltpu.VMEM((hb, blk, 2 * HEAD_DIM), F32)],
        compiler_params=_params("parallel", "parallel", "arbitrary"),
        name="moba_attn",
    )(qz, qz, kaug, vaug, kmean)


def _moba_layer(x, w_in, w_out, ln_g, ln_b, kaug, vaug, kmean, batch, seq):
    (qz,) = _project(x, [w_in.astype(BF16)], [F32], 1024, "moba_in_proj")
    o = _moba_attn(qz, kaug, vaug, kmean, batch, seq)
    return _out_proj_ln(o, w_out.astype(BF16), x, ln_g[None, :], ln_b[None, :], 512, "moba_out_ln")


def kernel(x, a_w_in, a_conv_w, a_A_log, a_dt_bias, a_norm_w, a_w_out, a_ln_g, a_ln_b, b_w_kv, b_w_in, b_w_out, b_ln_g, b_ln_b):
    batch, seq, d_model = x.shape
    assert d_model == D_MODEL and seq % (MOBA_GROUP * MOBA_BLOCK) == 0 and seq // MOBA_BLOCK <= HEAD_DIM
    assert a_w_in.shape[2] == 4 * HEADS * HEAD_DIM + 2 * HEADS and a_conv_w.shape[1] == CONV_K
    h = x.reshape(batch * seq, d_model)
    for i in range(a_w_in.shape[0]):
        h = _gdn_layer(h, a_w_in[i], a_conv_w[i], a_A_log[i], a_dt_bias[i], a_norm_w[i], a_w_out[i],
                       a_ln_g[i], a_ln_b[i], batch, seq)
    kaug, vaug, kmean = _shared_kv(h, b_w_kv.astype(BF16), seq)
    for i in range(b_w_in.shape[0]):
        h = _moba_layer(h, b_w_in[i], b_w_out[i], b_ln_g[i], b_ln_b[i], kaug, vaug, kmean, batch, seq)
    return h.reshape(batch, seq, d_model)
```

```python
import functools

import jax
import jax.numpy as jnp
from jax import lax
from jax.experimental import pallas as pl
from jax.experimental.pallas import tpu as pltpu

F32 = jnp.float32
BF16 = jnp.bfloat16

D_MODEL = 1024
HEADS = 8
HEAD_DIM = 128
CONV_K = 4
GDN_CHUNK = 64
GDN_BLOCK = 256
MOBA_BLOCK = 256
MOBA_TOP_K = 3
MOBA_GROUP = 4
MOBA_HEADS_PER_STEP = 4
DEPTH = 4
DEEPNORM_ALPHA = (2 * DEPTH) ** 0.25
LN_EPS = 1e-5
RMS_EPS = 1e-6
MASK_VALUE = -1e30
GATE_LANES = 128
V7X_VMEM_LIMIT = 56 * 1024 * 1024
PROJ_COL_TILE = 1024
OUT_LN_PARTS = 4

_NT = (((1,), (1,)), ((), ()))


def _params(*sem):
    return pltpu.CompilerParams(dimension_semantics=sem, vmem_limit_bytes=V7X_VMEM_LIMIT)


def _silu(x):
    return x * jax.nn.sigmoid(x)


def _proj_kernel(x_ref, *refs):
    nw = len(refs) // 2
    xb = x_ref[...].astype(BF16)
    for w_ref, o_ref in zip(refs[:nw], refs[nw:]):
        n = w_ref.shape[1]
        tn = min(n, PROJ_COL_TILE)
        for c in range(n // tn):
            o_ref[:, c * tn:(c + 1) * tn] = jnp.dot(xb, w_ref[:, c * tn:(c + 1) * tn],
                                                    preferred_element_type=F32).astype(o_ref.dtype)


def _project(x, weights, out_dtypes, tm, name):
    m, k = x.shape
    return pl.pallas_call(
        _proj_kernel,
        grid=(m // tm,),
        in_specs=[pl.BlockSpec((tm, k), lambda i: (i, 0))]
        + [pl.BlockSpec(w.shape, lambda i: (0, 0)) for w in weights],
        out_specs=[pl.BlockSpec((tm, w.shape[1]), lambda i: (i, 0)) for w in weights],
        out_shape=[jax.ShapeDtypeStruct((m, w.shape[1]), dt) for w, dt in zip(weights, out_dtypes)],
        compiler_params=_params("parallel"),
        name=name,
    )(x, *weights)


def _deepnorm_rows(x_ref, y_ref, g_ref, b_ref, out_ref, start, stop):
    rows = 64
    for r in range(start // rows, stop // rows):
        rs = slice(r * rows, (r + 1) * rows)
        s = DEEPNORM_ALPHA * x_ref[rs, :] + y_ref[rs, :]
        mu = jnp.mean(s, axis=-1, keepdims=True)
        c = s - mu
        var = jnp.mean(c * c, axis=-1, keepdims=True)
        out_ref[rs, :] = c * lax.rsqrt(var + LN_EPS) * g_ref[...] + b_ref[...]


def _out_ln_kernel(o_ref, w_ref, x_ref, g_ref, b_ref, out_ref, y_ref):
    parts = OUT_LN_PARTS
    size = o_ref.shape[0] // parts

    def matmul_part(i):
        y_ref[i * size:(i + 1) * size, :] = jnp.dot(o_ref[i * size:(i + 1) * size, :], w_ref[...],
                                                    preferred_element_type=F32)

    matmul_part(0)
    for i in range(parts):
        if i + 1 < parts:
            matmul_part(i + 1)
        _deepnorm_rows(x_ref, y_ref, g_ref, b_ref, out_ref, i * size, (i + 1) * size)


def _out_proj_ln(o, w, x, g, b, tm, name):
    m, k = o.shape
    n = w.shape[1]
    return pl.pallas_call(
        _out_ln_kernel,
        grid=(m // tm,),
        in_specs=[pl.BlockSpec((tm, k), lambda i: (i, 0)),
                  pl.BlockSpec((k, n), lambda i: (0, 0)),
                  pl.BlockSpec((tm, n), lambda i: (i, 0)),
                  pl.BlockSpec((1, n), lambda i: (0, 0)),
                  pl.BlockSpec((1, n), lambda i: (0, 0))],
        out_specs=pl.BlockSpec((tm, n), lambda i: (i, 0)),
        out_shape=jax.ShapeDtypeStruct((m, n), F32),
        scratch_shapes=[pltpu.VMEM((tm, n), F32)],
        compiler_params=_params("parallel"),
        name=name,
    )(o, w, x, g, b)


def _gdn_prep_kernel(q_ref, k_ref, v_ref, qp_ref, kp_ref, vp_ref, gates_ref, cw_ref, alog_ref, dtb_ref,
                     wq_ref, u_ref, kdt_ref, attn_ref, egl_ref, xs_ref, p_ref, tm_ref, rhs_ref):
    tb = GDN_BLOCK
    ck = GDN_CHUNK
    nck = tb // ck
    first = pl.program_id(1) == 0
    attn_ref[...] = jnp.zeros_like(attn_ref)

    gts = gates_ref[...]
    beta_all = jax.nn.sigmoid(gts)
    xg = gts + dtb_ref[...]
    softplus = jnp.maximum(xg, 0.0) + jnp.log1p(jnp.exp(-jnp.abs(xg)))
    g_all = -jnp.exp(alog_ref[...]) * softplus
    row = lax.broadcasted_iota(jnp.int32, (tb, GATE_LANES), 0)
    rin = jnp.bitwise_and(row, GDN_CHUNK - 1)
    gc = g_all
    s = 1
    while s < GDN_CHUNK:
        gc = gc + jnp.where(rin >= s, pltpu.roll(gc, s, axis=0), 0.0)
        s *= 2
    gc3 = gc.reshape(nck, GDN_CHUNK, GATE_LANES)
    gl = jnp.broadcast_to(gc3[:, GDN_CHUNK - 1:GDN_CHUNK, :], gc3.shape).reshape(tb, GATE_LANES)
    egc = jnp.exp(gc)
    ekd = jnp.exp(gl - gc)
    egl_ref[...] = jnp.exp(gl)
    gct = gc.T

    ri = lax.broadcasted_iota(jnp.int32, (ck, ck), 0)
    ci = lax.broadcasted_iota(jnp.int32, (ck, ck), 1)
    causal = ri >= ci
    strict = ri > ci
    eye = (ri == ci).astype(F32)
    eye_b = eye.astype(BF16)

    def conv_silu(x_ref, p_ref, slot, h, col0):
        sl = slice(h * HEAD_DIM, (h + 1) * HEAD_DIM)
        xs_ref[slot, 0:8, :] = jnp.where(first, 0.0, p_ref[:, sl])
        xs_ref[slot, 8:8 + tb, :] = x_ref[:, sl]
        cw = cw_ref[:, col0 + h * HEAD_DIM:col0 + (h + 1) * HEAD_DIM]
        acc = xs_ref[slot, 8:8 + tb, :] * cw[CONV_K - 1:CONV_K, :]
        for d in range(1, CONV_K):
            acc = acc + xs_ref[slot, 8 - d:8 - d + tb, :] * cw[CONV_K - 1 - d:CONV_K - d, :]
        return _silu(acc)

    for h in range(HEADS):
        sl = slice(h * HEAD_DIM, (h + 1) * HEAD_DIM)
        qc = conv_silu(q_ref, qp_ref, 3 * h, h, 0)
        kc = conv_silu(k_ref, kp_ref, 3 * h + 1, h, HEADS * HEAD_DIM)
        vc = conv_silu(v_ref, vp_ref, 3 * h + 2, h, 2 * HEADS * HEAD_DIM)
        qn = qc * lax.rsqrt(jnp.sum(qc * qc, axis=-1, keepdims=True) + RMS_EPS)
        kn = kc * lax.rsqrt(jnp.sum(kc * kc, axis=-1, keepdims=True) + RMS_EPS)
        beta = beta_all[:, h:h + 1]
        gcol = gc[:, HEADS + h:HEADS + h + 1]
        egc_c = egc[:, HEADS + h:HEADS + h + 1]
        ekd_c = ekd[:, HEADS + h:HEADS + h + 1]
        grow = gct[HEADS + h:HEADS + h + 1, :]

        qs = qn * (HEAD_DIM ** -0.5)
        kb = kn * beta
        kn_b = kn.astype(BF16)
        kb_b = kb.astype(BF16)
        qs_b = qs.astype(BF16)
        for c in range(nck):
            rs = slice(c * ck, (c + 1) * ck)
            kk = lax.dot_general(kb_b[rs, :], kn_b[rs, :], _NT, preferred_element_type=F32)
            qk = lax.dot_general(qs_b[rs, :], kn_b[rs, :], _NT, preferred_element_type=F32)
            dm = jnp.exp(jnp.where(causal, gcol[rs, :] - grow[:, rs], -jnp.inf))
            attn_ref[rs, h * HEAD_DIM:h * HEAD_DIM + ck] = (qk * dm).astype(BF16)
            nm = jnp.where(strict, -(kk * dm), 0.0)
            p_ref[h, c] = nm.astype(BF16)
            tm_ref[h, c] = (eye + nm).astype(BF16)
        rhs_ref[h, :, :HEAD_DIM] = (kb * egc_c).astype(BF16)
        rhs_ref[h, :, HEAD_DIM:] = (vc * beta).astype(BF16)
        qd = (qs * egc_c).astype(BF16)
        for c in range(nck):
            wq_ref[(2 * c + 1) * GDN_CHUNK:(2 * c + 2) * GDN_CHUNK, sl] = qd[c * GDN_CHUNK:(c + 1) * GDN_CHUNK, :]
        kdt_ref[0, sl, :] = (kn * ekd_c).T.astype(BF16)

    pairs = [(h, c) for h in range(HEADS) for c in range(nck)]
    for _ in range(GDN_CHUNK.bit_length() - 2):
        for h, c in pairs:
            p = p_ref[h, c]
            p_ref[h, c] = jnp.dot(p, p, preferred_element_type=F32).astype(BF16)
        for h, c in pairs:
            tm_ref[h, c] = jnp.dot(tm_ref[h, c], p_ref[h, c] + eye_b, preferred_element_type=F32).astype(BF16)
    for h, c in pairs:
        sl = slice(h * HEAD_DIM, (h + 1) * HEAD_DIM)
        rs = slice(c * ck, (c + 1) * ck)
        wu = jnp.dot(tm_ref[h, c], rhs_ref[h, rs, :], preferred_element_type=F32)
        wq_ref[2 * c * ck:(2 * c + 1) * ck, sl] = wu[:, :HEAD_DIM].astype(BF16)
        u_ref[rs, sl] = wu[:, HEAD_DIM:]


def _gdn_prep(proj, gates, conv_w, alog_row, dtb_row, batch, seq):
    tb = GDN_BLOCK
    nt = seq // tb
    bt = batch * seq
    width = HEADS * HEAD_DIM
    blk = lambda col: pl.BlockSpec((tb, width), lambda b, t: (b * nt + t, col))
    prev = lambda col: pl.BlockSpec((8, width), lambda b, t: (jnp.maximum((b * nt + t) * (tb // 8) - 1, 0), col))
    return pl.pallas_call(
        _gdn_prep_kernel,
        grid=(batch, nt),
        in_specs=[blk(0), blk(1), blk(2), prev(0), prev(1), prev(2),
                  pl.BlockSpec((tb, GATE_LANES), lambda b, t: (b * nt + t, 0)),
                  pl.BlockSpec((CONV_K, 3 * width), lambda b, t: (0, 0)),
                  pl.BlockSpec((1, GATE_LANES), lambda b, t: (0, 0)),
                  pl.BlockSpec((1, GATE_LANES), lambda b, t: (0, 0))],
        out_specs=[pl.BlockSpec((2 * tb, width), lambda b, t: (b * nt + t, 0)),
                   pl.BlockSpec((tb, width), lambda b, t: (b * nt + t, 0)),
                   pl.BlockSpec((1, width, tb), lambda b, t: (b * nt + t, 0, 0)),
                   pl.BlockSpec((tb, width), lambda b, t: (b * nt + t, 0)),
                   pl.BlockSpec((tb, GATE_LANES), lambda b, t: (b * nt + t, 0))],
        out_shape=[jax.ShapeDtypeStruct((2 * bt, width), BF16),
                   jax.ShapeDtypeStruct((bt, width), F32),
                   jax.ShapeDtypeStruct((bt // tb, width, tb), BF16),
                   jax.ShapeDtypeStruct((bt, width), BF16),
                   jax.ShapeDtypeStruct((bt, GATE_LANES), F32)],
        scratch_shapes=[pltpu.VMEM((3 * HEADS, tb + 8, HEAD_DIM), F32),
                        pltpu.VMEM((HEADS, tb // GDN_CHUNK, GDN_CHUNK, GDN_CHUNK), BF16),
                        pltpu.VMEM((HEADS, tb // GDN_CHUNK, GDN_CHUNK, GDN_CHUNK), BF16),
                        pltpu.VMEM((HEADS, tb, 2 * HEAD_DIM), BF16)],
        compiler_params=_params("parallel", "arbitrary"),
        name="gdn_prep",
    )(proj, proj, proj, proj, proj, proj, gates, conv_w, alog_row, dtb_row)


def _gdn_scan_kernel(wq_ref, u_ref, kdt_ref, attn_ref, egl_ref, z_ref, nw_ref, wo_ref, x_ref, g_ref, b_ref,
                     out_ref, s_ref, vn_ref, qs_ref, o_ref, y_ref):
    tb = GDN_BLOCK
    nck = tb // GDN_CHUNK
    ck = GDN_CHUNK

    @pl.when(pl.program_id(1) == 0)
    def _():
        s_ref[...] = jnp.zeros_like(s_ref)

    zeros_half = jnp.zeros((ck, HEAD_DIM), BF16)
    for c in range(nck):
        rs = slice(c * ck, (c + 1) * ck)
        for h in range(HEADS):
            sl = slice(h * HEAD_DIM, (h + 1) * HEAD_DIM)
            state = s_ref[h]
            r = jnp.dot(wq_ref[2 * c * ck:(2 * c + 2) * ck, sl], state.astype(BF16),
                        preferred_element_type=F32)
            vnew = (u_ref[rs, sl] - r[:ck, :]).astype(BF16)
            qs_ref[h, rs, :] = r[ck:, :]
            vn_ref[h, rs, :] = vnew
            vpad = jnp.concatenate([vnew, zeros_half] if c % 2 == 0 else [zeros_half, vnew], axis=0)
            slab = kdt_ref[0, sl, (c // 2) * 2 * ck:(c // 2 + 1) * 2 * ck]
            cd = egl_ref[c * ck:c * ck + 1, HEADS + h:HEADS + h + 1]
            s_ref[h] = state * cd + jnp.dot(slab, vpad, preferred_element_type=F32)
    for h in range(HEADS):
        sl = slice(h * HEAD_DIM, (h + 1) * HEAD_DIM)
        for c in range(nck):
            rs = slice(c * ck, (c + 1) * ck)
            o = qs_ref[h, rs, :] + jnp.dot(attn_ref[rs, h * HEAD_DIM:h * HEAD_DIM + ck], vn_ref[h, rs, :],
                                           preferred_element_type=F32)
            o = o * lax.rsqrt(jnp.mean(o * o, axis=-1, keepdims=True) + RMS_EPS) * nw_ref[...]
            o_ref[rs, sl] = (o * _silu(z_ref[rs, sl])).astype(BF16)
    y_ref[...] = jnp.dot(o_ref[...], wo_ref[...], preferred_element_type=F32)
    _deepnorm_rows(x_ref, y_ref, g_ref, b_ref, out_ref, 0, tb)


def _gdn_scan(wq, u, kdt, attn, egl, proj, norm_w, w_out, x, ln_g, ln_b, batch, seq):
    tb = GDN_BLOCK
    nt = seq // tb
    bt = batch * seq
    width = HEADS * HEAD_DIM
    return pl.pallas_call(
        _gdn_scan_kernel,
        grid=(batch, nt),
        in_specs=[pl.BlockSpec((2 * tb, width), lambda b, t: (b * nt + t, 0)),
                  pl.BlockSpec((tb, width), lambda b, t: (b * nt + t, 0)),
                  pl.BlockSpec((1, width, tb), lambda b, t: (b * nt + t, 0, 0)),
                  pl.BlockSpec((tb, width), lambda b, t: (b * nt + t, 0)),
                  pl.BlockSpec((tb, GATE_LANES), lambda b, t: (b * nt + t, 0)),
                  pl.BlockSpec((tb, width), lambda b, t: (b * nt + t, 3)),
                  pl.BlockSpec((1, HEAD_DIM), lambda b, t: (0, 0)),
                  pl.BlockSpec((width, D_MODEL), lambda b, t: (0, 0)),
                  pl.BlockSpec((tb, D_MODEL), lambda b, t: (b * nt + t, 0)),
                  pl.BlockSpec((1, D_MODEL), lambda b, t: (0, 0)),
                  pl.BlockSpec((1, D_MODEL), lambda b, t: (0, 0))],
        out_specs=pl.BlockSpec((tb, D_MODEL), lambda b, t: (b * nt + t, 0)),
        out_shape=jax.ShapeDtypeStruct((bt, D_MODEL), F32),
        scratch_shapes=[pltpu.VMEM((HEADS, HEAD_DIM, HEAD_DIM), F32),
                        pltpu.VMEM((HEADS, tb, HEAD_DIM), BF16),
                        pltpu.VMEM((HEADS, tb, HEAD_DIM), F32),
                        pltpu.VMEM((tb, width), BF16),
                        pltpu.VMEM((tb, D_MODEL), F32)],
        compiler_params=_params("parallel", "arbitrary"),
        name="gdn_scan",
    )(wq, u, kdt, attn, egl, proj, norm_w, w_out, x, ln_g, ln_b)


def _gdn_layer(x, w_in, conv_w, a_log, dt_bias, norm_w, w_out, ln_g, ln_b, batch, seq):
    width = HEADS * HEAD_DIM
    w_main = w_in[:, :4 * width].astype(BF16)
    w_gate = jnp.pad(w_in[:, 4 * width:], ((0, 0), (0, GATE_LANES - 2 * HEADS))).astype(BF16)
    pad_row = lambda v: jnp.pad(v.astype(F32), (HEADS, GATE_LANES - 2 * HEADS))[None, :]
    proj, gates = _project(x, [w_main, w_gate], [F32, F32], 512, "gdn_in_proj")
    wq, u, kdt, attn, egl = _gdn_prep(proj, gates, conv_w.astype(F32), pad_row(a_log), pad_row(dt_bias), batch, seq)
    return _gdn_scan(wq, u, kdt, attn, egl, proj, norm_w.astype(F32)[None, :],
                     w_out.astype(BF16), x, ln_g[None, :], ln_b[None, :], batch, seq)


def _kv_kernel(x_ref, w_ref, kaug_ref, vaug_ref, kmean_ref, *, nq):
    blk = MOBA_BLOCK
    width = HEADS * HEAD_DIM
    aug = 2 * HEAD_DIM
    nblk = x_ref.shape[0] // blk
    lane = lax.broadcasted_iota(jnp.int32, (blk, HEAD_DIM), 1)
    ones_col = (lane == 0).astype(BF16)
    for r in range(nblk):
        rs = slice(r * blk, (r + 1) * blk)
        acc = jnp.dot(x_ref[rs, :].astype(BF16), w_ref[...], preferred_element_type=F32)
        onehot = (lane == lax.rem(pl.program_id(0) * nblk + r, nq)).astype(BF16)
        for h in range(HEADS):
            kaug_ref[rs, h * aug:h * aug + HEAD_DIM] = acc[:, h * HEAD_DIM:(h + 1) * HEAD_DIM].astype(BF16)
            kaug_ref[rs, h * aug + HEAD_DIM:(h + 1) * aug] = onehot
            vaug_ref[rs, h * aug:h * aug + HEAD_DIM] = acc[:, width + h * HEAD_DIM:width + (h + 1) * HEAD_DIM].astype(BF16)
            vaug_ref[rs, h * aug + HEAD_DIM:(h + 1) * aug] = ones_col
        row = lax.rem(pl.program_id(0), 8 // nblk) * nblk + r
        kmean_ref[pl.ds(row, 1), :] = jnp.mean(acc[:, :width], axis=0, keepdims=True)


def _shared_kv(x, w_kv, seq):
    m, k = x.shape
    n = w_kv.shape[1]
    tm = 4 * MOBA_BLOCK
    steps_per_mean_block = 8 * MOBA_BLOCK // tm
    return pl.pallas_call(
        functools.partial(_kv_kernel, nq=seq // MOBA_BLOCK),
        grid=(m // tm,),
        in_specs=[pl.BlockSpec((tm, k), lambda i: (i, 0)),
                  pl.BlockSpec((k, n), lambda i: (0, 0))],
        out_specs=[pl.BlockSpec((tm, n), lambda i: (i, 0)),
                   pl.BlockSpec((tm, n), lambda i: (i, 0)),
                   pl.BlockSpec((8, n // 2), lambda i: (i // steps_per_mean_block, 0))],
        out_shape=[jax.ShapeDtypeStruct((m, n), BF16),
                   jax.ShapeDtypeStruct((m, n), BF16),
                   jax.ShapeDtypeStruct((m // MOBA_BLOCK, n // 2), F32)],
        compiler_params=_params("arbitrary"),
        name="moba_shared_kv",
    )(x, w_kv)


def _moba_select_bias(q_b, kmean, qb):
    nb = kmean.shape[0]
    blk = q_b.shape[0]
    km = jnp.concatenate([kmean, jnp.zeros((HEAD_DIM - nb, HEAD_DIM), F32)], axis=0).astype(BF16)
    gate_t = lax.dot_general(km, q_b, _NT, preferred_element_type=F32)[:nb]
    jrow = lax.broadcasted_iota(jnp.int32, (nb, blk), 0)
    valid = jrow < qb
    jf = jrow.astype(F32)
    g = jnp.where(valid, gate_t, -jnp.inf)
    keep = jnp.zeros((nb, blk), F32)
    for _ in range(MOBA_TOP_K):
        mx = jnp.max(g, axis=0, keepdims=True)
        first = jnp.min(jnp.where(g == mx, jf, float(nb)), axis=0, keepdims=True)
        sel = jf == first
        keep = jnp.where(sel, 1.0, keep)
        g = jnp.where(sel, -jnp.inf, g)
    bias_t = jnp.where(valid & (keep > 0.0), 0.0, MASK_VALUE)
    bias_t = jnp.concatenate([bias_t, jnp.zeros((HEAD_DIM - nb, blk), F32)], axis=0)
    return bias_t.T


def _moba_attn_kernel(q_ref, z_ref, k_ref, v_ref, kmean_ref, o_ref, acc_ref):
    blk = MOBA_BLOCK
    grp = MOBA_GROUP
    aug = 2 * HEAD_DIM
    qb = pl.program_id(2)
    own = pl.multiple_of(qb * blk, blk)
    qi = lax.broadcasted_iota(jnp.int32, (blk, blk), 0)
    ki = lax.broadcasted_iota(jnp.int32, (blk, blk), 1)
    causal = ki <= qi
    zeros_b = jnp.zeros((blk, HEAD_DIM), BF16)

    heads = range(MOBA_HEADS_PER_STEP)
    ha = [slice(h * aug, (h + 1) * aug) for h in heads]
    q_bs = [(q_ref[:, h * HEAD_DIM:(h + 1) * HEAD_DIM] * (HEAD_DIM ** -0.5)).astype(BF16) for h in heads]
    q_augs = [jnp.concatenate([q_bs[h], _moba_select_bias(
        q_bs[h], kmean_ref[:, h * HEAD_DIM:(h + 1) * HEAD_DIM], qb).astype(BF16)], axis=1) for h in heads]
    s_own = [jnp.where(causal, lax.dot_general(jnp.concatenate([q_bs[h], zeros_b], axis=1),
                                               k_ref[pl.ds(own, blk), ha[h]], _NT, preferred_element_type=F32),
                       MASK_VALUE) for h in heads]
    s_rest = [lax.dot_general(q_augs[h], k_ref[0:(grp - 1) * blk, ha[h]], _NT, preferred_element_type=F32)
              for h in heads]
    m_init = [jnp.maximum(jnp.max(s_own[h], axis=-1, keepdims=True), jnp.max(s_rest[h], axis=-1, keepdims=True))
              for h in heads]
    for h in heads:
        acc_ref[h] = (jnp.dot(jnp.exp(s_own[h] - m_init[h]).astype(BF16), v_ref[pl.ds(own, blk), ha[h]],
                              preferred_element_type=F32)
                      + jnp.dot(jnp.exp(s_rest[h] - m_init[h]).astype(BF16), v_ref[0:(grp - 1) * blk, ha[h]],
                                preferred_element_type=F32))

    def group_step(start, ms):
        s = [lax.dot_general(q_augs[h], k_ref[pl.ds(start, grp * blk), ha[h]], _NT, preferred_element_type=F32)
             for h in heads]
        m_new = [jnp.maximum(ms[h], jnp.max(s[h], axis=-1, keepdims=True)) for h in heads]
        p = [jnp.exp(s[h] - m_new[h]).astype(BF16) for h in heads]
        for h in heads:
            acc_ref[h] = jnp.exp(ms[h] - m_new[h]) * acc_ref[h] + jnp.dot(
                p[h], v_ref[pl.ds(start, grp * blk), ha[h]], preferred_element_type=F32)
        return tuple(m_new)

    def group_start(g):
        return pl.multiple_of((grp - 1 + g * grp) * blk, blk)

    n_groups = (jnp.maximum(qb - (grp - 1), 0) + grp - 1) // grp
    ms = lax.fori_loop(0, n_groups // 2,
                       lambda i, ms: group_step(group_start(2 * i + 1), group_step(group_start(2 * i), ms)),
                       tuple(m_init))

    @pl.when(lax.rem(n_groups, 2) == 1)
    def _():
        group_step(group_start(n_groups - 1), ms)

    for h in heads:
        hs = slice(h * HEAD_DIM, (h + 1) * HEAD_DIM)
        acc = acc_ref[h]
        o = acc[:, :HEAD_DIM] / acc[:, HEAD_DIM:HEAD_DIM + 1]
        o_ref[:, hs] = (o * _silu(z_ref[:, hs])).astype(BF16)


def _moba_attn(qz, kaug, vaug, kmean, batch, seq):
    blk = MOBA_BLOCK
    nq = seq // blk
    bt = batch * seq
    hb = MOBA_HEADS_PER_STEP
    groups = HEADS // hb
    assert nq % MOBA_GROUP == 0
    return pl.pallas_call(
        _moba_attn_kernel,
        grid=(batch, groups, nq),
        in_specs=[pl.BlockSpec((blk, hb * HEAD_DIM), lambda b, h, i: (b * nq + i, h)),
                  pl.BlockSpec((blk, hb * HEAD_DIM), lambda b, h, i: (b * nq + i, groups + h)),
                  pl.BlockSpec((seq, hb * 2 * HEAD_DIM), lambda b, h, i: (b, h), pipeline_mode=pl.Buffered(1)),
                  pl.BlockSpec((seq, hb * 2 * HEAD_DIM), lambda b, h, i: (b, h), pipeline_mode=pl.Buffered(1)),
                  pl.BlockSpec((nq, hb * HEAD_DIM), lambda b, h, i: (b, h))],
        out_specs=pl.BlockSpec((blk, hb * HEAD_DIM), lambda b, h, i: (b * nq + i, h)),
        out_shape=jax.ShapeDtypeStruct((bt, HEADS * HEAD_DIM), BF16),
        scratch_shapes=[pltpu.VMEM((hb, blk, 2 * HEAD_DIM), F32)],
        compiler_params=_params("parallel", "parallel", "arbitrary"),
        name="moba_attn",
    )(qz, qz, kaug, vaug, kmean)


def _moba_layer(x, w_in, w_out, ln_g, ln_b, kaug, vaug, kmean, batch, seq):
    (qz,) = _project(x, [w_in.astype(BF16)], [F32], 1024, "moba_in_proj")
    o = _moba_attn(qz, kaug, vaug, kmean, batch, seq)
    return _out_proj_ln(o, w_out.astype(BF16), x, ln_g[None, :], ln_b[None, :], 512, "moba_out_ln")


def kernel(x, a_w_in, a_conv_w, a_A_log, a_dt_bias, a_norm_w, a_w_out, a_ln_g, a_ln_b, b_w_kv, b_w_in, b_w_out, b_ln_g, b_ln_b):
    batch, seq, d_model = x.shape
    assert d_model == D_MODEL and seq % (MOBA_GROUP * MOBA_BLOCK) == 0 and seq // MOBA_BLOCK <= HEAD_DIM
    assert a_w_in.shape[2] == 4 * HEADS * HEAD_DIM + 2 * HEADS and a_conv_w.shape[1] == CONV_K
    h = x.reshape(batch * seq, d_model)
    for i in range(a_w_in.shape[0]):
        h = _gdn_layer(h, a_w_in[i], a_conv_w[i], a_A_log[i], a_dt_bias[i], a_norm_w[i], a_w_out[i],
                       a_ln_g[i], a_ln_b[i], batch, seq)
    kaug, vaug, kmean = _shared_kv(h, b_w_kv.astype(BF16), seq)
    for i in range(b_w_in.shape[0]):
        h = _moba_layer(h, b_w_in[i], b_w_out[i], b_ln_g[i], b_ln_b[i], kaug, vaug, kmean, batch, seq)
    return h.reshape(batch, seq, d_model)
```

```python
import functools

import jax
import jax.numpy as jnp
from jax import lax
from jax.experimental import pallas as pl
from jax.experimental.pallas import tpu as pltpu

F32 = jnp.float32
BF16 = jnp.bfloat16

D_MODEL = 1024
HEADS = 8
HEAD_DIM = 128
CONV_K = 4
GDN_CHUNK = 64
GDN_BLOCK = 256
MOBA_BLOCK = 256
MOBA_TOP_K = 3
MOBA_GROUP = 4
MOBA_HEADS_PER_STEP = 4
DEPTH = 4
DEEPNORM_ALPHA = (2 * DEPTH) ** 0.25
LN_EPS = 1e-5
RMS_EPS = 1e-6
MASK_VALUE = -1e30
GATE_LANES = 128
V7X_VMEM_LIMIT = 56 * 1024 * 1024
PROJ_COL_TILE = 1024
OUT_LN_PARTS = 4

_NT = (((1,), (1,)), ((), ()))


def _params(*sem):
    return pltpu.CompilerParams(dimension_semantics=sem, vmem_limit_bytes=V7X_VMEM_LIMIT)


def _silu(x):
    return x * jax.nn.sigmoid(x)


def _proj_kernel(x_ref, *refs):
    nw = len(refs) // 2
    xb = x_ref[...].astype(BF16)
    for w_ref, o_ref in zip(refs[:nw], refs[nw:]):
        n = w_ref.shape[1]
        tn = min(n, PROJ_COL_TILE)
        for c in range(n // tn):
            o_ref[:, c * tn:(c + 1) * tn] = jnp.dot(xb, w_ref[:, c * tn:(c + 1) * tn],
                                                    preferred_element_type=F32).astype(o_ref.dtype)


def _project(x, weights, out_dtypes, tm, name):
    m, k = x.shape
    return pl.pallas_call(
        _proj_kernel,
        grid=(m // tm,),
        in_specs=[pl.BlockSpec((tm, k), lambda i: (i, 0))]
        + [pl.BlockSpec(w.shape, lambda i: (0, 0)) for w in weights],
        out_specs=[pl.BlockSpec((tm, w.shape[1]), lambda i: (i, 0)) for w in weights],
        out_shape=[jax.ShapeDtypeStruct((m, w.shape[1]), dt) for w, dt in zip(weights, out_dtypes)],
        compiler_params=_params("parallel"),
        name=name,
    )(x, *weights)


def _deepnorm_rows(x_ref, y_ref, g_ref, b_ref, out_ref, start, stop):
    rows = 64
    for r in range(start // rows, stop // rows):
        rs = slice(r * rows, (r + 1) * rows)
        s = DEEPNORM_ALPHA * x_ref[rs, :] + y_ref[rs, :]
        mu = jnp.mean(s, axis=-1, keepdims=True)
        c = s - mu
        var = jnp.mean(c * c, axis=-1, keepdims=True)
        out_ref[rs, :] = c * lax.rsqrt(var + LN_EPS) * g_ref[...] + b_ref[...]


def _out_ln_kernel(o_ref, w_ref, x_ref, g_ref, b_ref, out_ref, y_ref):
    parts = OUT_LN_PARTS
    size = o_ref.shape[0] // parts

    def matmul_part(i):
        y_ref[i * size:(i + 1) * size, :] = jnp.dot(o_ref[i * size:(i + 1) * size, :], w_ref[...],
                                                    preferred_element_type=F32)

    matmul_part(0)
    for i in range(parts):
        if i + 1 < parts:
            matmul_part(i + 1)
        _deepnorm_rows(x_ref, y_ref, g_ref, b_ref, out_ref, i * size, (i + 1) * size)


def _out_proj_ln(o, w, x, g, b, tm, name):
    m, k = o.shape
    n = w.shape[1]
    return pl.pallas_call(
        _out_ln_kernel,
        grid=(m // tm,),
        in_specs=[pl.BlockSpec((tm, k), lambda i: (i, 0)),
                  pl.BlockSpec((k, n), lambda i: (0, 0)),
                  pl.BlockSpec((tm, n), lambda i: (i, 0)),
                  pl.BlockSpec((1, n), lambda i: (0, 0)),
                  pl.BlockSpec((1, n), lambda i: (0, 0))],
        out_specs=pl.BlockSpec((tm, n), lambda i: (i, 0)),
        out_shape=jax.ShapeDtypeStruct((m, n), F32),
        scratch_shapes=[pltpu.VMEM((tm, n), F32)],
        compiler_params=_params("parallel"),
        name=name,
    )(o, w, x, g, b)


def _gdn_prep_kernel(q_ref, k_ref, v_ref, qp_ref, kp_ref, vp_ref, gates_ref, cw_ref, alog_ref, dtb_ref,
                     wq_ref, u_ref, kdt_ref, attn_ref, egl_ref, xs_ref, p_ref, tm_ref, rhs_ref):
    tb = GDN_BLOCK
    ck = GDN_CHUNK
    nck = tb // ck
    first = pl.program_id(1) == 0
    attn_ref[...] = jnp.zeros_like(attn_ref)

    gate_t = gates_ref[...].T[:2 * HEADS, :]
    wide = lambda r: jnp.concatenate([r[...]] * (tb // GATE_LANES), axis=1)
    beta_t = jax.nn.sigmoid(gate_t[:HEADS, :])
    xg = gate_t[HEADS:, :] + wide(dtb_ref)
    g_t = -jnp.exp(wide(alog_ref)) * (jnp.maximum(xg, 0.0) + jnp.log1p(jnp.exp(-jnp.abs(xg))))
    lin = jnp.bitwise_and(lax.broadcasted_iota(jnp.int32, (HEADS, tb), 1), ck - 1)
    gc_t = g_t
    rest = g_t
    s = 1
    while s < ck:
        gc_t = gc_t + jnp.where(lin >= s, pltpu.roll(gc_t, s, axis=1), 0.0)
        rest = rest + jnp.where(lin < ck - s, pltpu.roll(rest, tb - s, axis=1), 0.0)
        s *= 2
    after = rest - g_t
    cols = jnp.concatenate([beta_t, gc_t, jnp.exp(gc_t), jnp.exp(after), jnp.exp(gc_t + after),
                            jnp.zeros((GATE_LANES - 5 * HEADS, tb), F32)], axis=0).T
    egl_ref[...] = cols

    ri = lax.broadcasted_iota(jnp.int32, (ck, ck), 0)
    ci = lax.broadcasted_iota(jnp.int32, (ck, ck), 1)
    causal = ri >= ci
    strict = ri > ci
    eye = (ri == ci).astype(F32)
    eye_b = eye.astype(BF16)

    def conv_silu(x_ref, p_ref, slot, h, col0):
        sl = slice(h * HEAD_DIM, (h + 1) * HEAD_DIM)
        xs_ref[slot, 0:8, :] = jnp.where(first, 0.0, p_ref[:, sl])
        xs_ref[slot, 8:8 + tb, :] = x_ref[:, sl]
        cw = cw_ref[:, col0 + h * HEAD_DIM:col0 + (h + 1) * HEAD_DIM]
        acc = xs_ref[slot, 8:8 + tb, :] * cw[CONV_K - 1:CONV_K, :]
        for d in range(1, CONV_K):
            acc = acc + xs_ref[slot, 8 - d:8 - d + tb, :] * cw[CONV_K - 1 - d:CONV_K - d, :]
        return _silu(acc)

    hs = range(HEADS)
    col = lambda group, h: cols[:, group * HEADS + h:group * HEADS + h + 1]
    qc = [conv_silu(q_ref, qp_ref, 3 * h, h, 0) for h in hs]
    kc = [conv_silu(k_ref, kp_ref, 3 * h + 1, h, HEADS * HEAD_DIM) for h in hs]
    vc = [conv_silu(v_ref, vp_ref, 3 * h + 2, h, 2 * HEADS * HEAD_DIM) for h in hs]
    qs = [qc[h] * (lax.rsqrt(jnp.sum(qc[h] * qc[h], axis=-1, keepdims=True) + RMS_EPS) * (HEAD_DIM ** -0.5))
          for h in hs]
    kn = [kc[h] * lax.rsqrt(jnp.sum(kc[h] * kc[h], axis=-1, keepdims=True) + RMS_EPS) for h in hs]
    kb = [kn[h] * col(0, h) for h in hs]
    kn_b = [kn[h].astype(BF16) for h in hs]
    kb_b = [kb[h].astype(BF16) for h in hs]
    qs_b = [qs[h].astype(BF16) for h in hs]
    pairs = [(h, c) for h in hs for c in range(nck)]
    rows = [slice(c * ck, (c + 1) * ck) for c in range(nck)]
    kk = {(h, c): lax.dot_general(kb_b[h][rows[c], :], kn_b[h][rows[c], :], _NT, preferred_element_type=F32)
          for h, c in pairs}
    qk = {(h, c): lax.dot_general(qs_b[h][rows[c], :], kn_b[h][rows[c], :], _NT, preferred_element_type=F32)
          for h, c in pairs}
    for h, c in pairs:
        dm = jnp.exp(jnp.where(causal, col(1, h)[rows[c], :] - gc_t[h:h + 1, rows[c]], -jnp.inf))
        attn_ref[rows[c], h * HEAD_DIM:h * HEAD_DIM + ck] = (qk[h, c] * dm).astype(BF16)
        nm = jnp.where(strict, -(kk[h, c] * dm), 0.0)
        p_ref[h, c] = nm.astype(BF16)
        tm_ref[h, c] = (eye + nm).astype(BF16)
    for h in hs:
        sl = slice(h * HEAD_DIM, (h + 1) * HEAD_DIM)
        rhs_ref[h, :, :HEAD_DIM] = (kb[h] * col(2, h)).astype(BF16)
        rhs_ref[h, :, HEAD_DIM:] = (vc[h] * col(0, h)).astype(BF16)
        qd = (qs[h] * col(2, h)).astype(BF16)
        for c in range(nck):
            wq_ref[(2 * c + 1) * ck:(2 * c + 2) * ck, sl] = qd[rows[c], :]
        kdt_ref[0, sl, :] = (kn[h] * col(3, h)).T.astype(BF16)

    pairs = [(h, c) for h in range(HEADS) for c in range(nck)]
    for _ in range(GDN_CHUNK.bit_length() - 2):
        for h, c in pairs:
            p = p_ref[h, c]
            p_ref[h, c] = jnp.dot(p, p, preferred_element_type=F32).astype(BF16)
        for h, c in pairs:
            tm_ref[h, c] = jnp.dot(tm_ref[h, c], p_ref[h, c] + eye_b, preferred_element_type=F32).astype(BF16)
    for h, c in pairs:
        sl = slice(h * HEAD_DIM, (h + 1) * HEAD_DIM)
        rs = slice(c * ck, (c + 1) * ck)
        wu = jnp.dot(tm_ref[h, c], rhs_ref[h, rs, :], preferred_element_type=F32)
        wq_ref[2 * c * ck:(2 * c + 1) * ck, sl] = wu[:, :HEAD_DIM].astype(BF16)
        u_ref[rs, sl] = wu[:, HEAD_DIM:]


def _gdn_prep(proj, gates, conv_w, alog_row, dtb_row, batch, seq):
    tb = GDN_BLOCK
    nt = seq // tb
    bt = batch * seq
    width = HEADS * HEAD_DIM
    blk = lambda col: pl.BlockSpec((tb, width), lambda b, t: (b * nt + t, col))
    prev = lambda col: pl.BlockSpec((8, width), lambda b, t: (jnp.maximum((b * nt + t) * (tb // 8) - 1, 0), col))
    return pl.pallas_call(
        _gdn_prep_kernel,
        grid=(batch, nt),
        in_specs=[blk(0), blk(1), blk(2), prev(0), prev(1), prev(2),
                  pl.BlockSpec((tb, GATE_LANES), lambda b, t: (b * nt + t, 0)),
                  pl.BlockSpec((CONV_K, 3 * width), lambda b, t: (0, 0)),
                  pl.BlockSpec((HEADS, GATE_LANES), lambda b, t: (0, 0)),
                  pl.BlockSpec((HEADS, GATE_LANES), lambda b, t: (0, 0))],
        out_specs=[pl.BlockSpec((2 * tb, width), lambda b, t: (b * nt + t, 0)),
                   pl.BlockSpec((tb, width), lambda b, t: (b * nt + t, 0)),
                   pl.BlockSpec((1, width, tb), lambda b, t: (b * nt + t, 0, 0)),
                   pl.BlockSpec((tb, width), lambda b, t: (b * nt + t, 0)),
                   pl.BlockSpec((tb, GATE_LANES), lambda b, t: (b * nt + t, 0))],
        out_shape=[jax.ShapeDtypeStruct((2 * bt, width), BF16),
                   jax.ShapeDtypeStruct((bt, width), F32),
                   jax.ShapeDtypeStruct((bt // tb, width, tb), BF16),
                   jax.ShapeDtypeStruct((bt, width), BF16),
                   jax.ShapeDtypeStruct((bt, GATE_LANES), F32)],
        scratch_shapes=[pltpu.VMEM((3 * HEADS, tb + 8, HEAD_DIM), F32),
                        pltpu.VMEM((HEADS, tb // GDN_CHUNK, GDN_CHUNK, GDN_CHUNK), BF16),
                        pltpu.VMEM((HEADS, tb // GDN_CHUNK, GDN_CHUNK, GDN_CHUNK), BF16),
                        pltpu.VMEM((HEADS, tb, 2 * HEAD_DIM), BF16)],
        compiler_params=_params("parallel", "arbitrary"),
        name="gdn_prep",
    )(proj, proj, proj, proj, proj, proj, gates, conv_w, alog_row, dtb_row)


def _gdn_scan_kernel(wq_ref, u_ref, kdt_ref, attn_ref, egl_ref, z_ref, nw_ref, wo_ref, x_ref, g_ref, b_ref,
                     out_ref, s_ref, vn_ref, qs_ref, o_ref, y_ref):
    tb = GDN_BLOCK
    nck = tb // GDN_CHUNK
    ck = GDN_CHUNK

    @pl.when(pl.program_id(1) == 0)
    def _():
        s_ref[...] = jnp.zeros_like(s_ref)

    zeros_half = jnp.zeros((ck, HEAD_DIM), BF16)
    for c in range(nck):
        rs = slice(c * ck, (c + 1) * ck)
        for h in range(HEADS):
            sl = slice(h * HEAD_DIM, (h + 1) * HEAD_DIM)
            state = s_ref[h]
            r = jnp.dot(wq_ref[2 * c * ck:(2 * c + 2) * ck, sl], state.astype(BF16),
                        preferred_element_type=F32)
            vnew = (u_ref[rs, sl] - r[:ck, :]).astype(BF16)
            qs_ref[h, rs, :] = r[ck:, :]
            vn_ref[h, rs, :] = vnew
            vpad = jnp.concatenate([vnew, zeros_half] if c % 2 == 0 else [zeros_half, vnew], axis=0)
            slab = kdt_ref[0, sl, (c // 2) * 2 * ck:(c // 2 + 1) * 2 * ck]
            cd = egl_ref[c * ck:c * ck + 1, 4 * HEADS + h:4 * HEADS + h + 1]
            s_ref[h] = state * cd + jnp.dot(slab, vpad, preferred_element_type=F32)
    for h in range(HEADS):
        sl = slice(h * HEAD_DIM, (h + 1) * HEAD_DIM)
        for c in range(nck):
            rs = slice(c * ck, (c + 1) * ck)
            o = qs_ref[h, rs, :] + jnp.dot(attn_ref[rs, h * HEAD_DIM:h * HEAD_DIM + ck], vn_ref[h, rs, :],
                                           preferred_element_type=F32)
            o = o * lax.rsqrt(jnp.mean(o * o, axis=-1, keepdims=True) + RMS_EPS) * nw_ref[...]
            o_ref[rs, sl] = (o * _silu(z_ref[rs, sl])).astype(BF16)
    y_ref[...] = jnp.dot(o_ref[...], wo_ref[...], preferred_element_type=F32)
    _deepnorm_rows(x_ref, y_ref, g_ref, b_ref, out_ref, 0, tb)


def _gdn_scan(wq, u, kdt, attn, egl, proj, norm_w, w_out, x, ln_g, ln_b, batch, seq):
    tb = GDN_BLOCK
    nt = seq // tb
    bt = batch * seq
    width = HEADS * HEAD_DIM
    return pl.pallas_call(
        _gdn_scan_kernel,
        grid=(batch, nt),
        in_specs=[pl.BlockSpec((2 * tb, width), lambda b, t: (b * nt + t, 0)),
                  pl.BlockSpec((tb, width), lambda b, t: (b * nt + t, 0)),
                  pl.BlockSpec((1, width, tb), lambda b, t: (b * nt + t, 0, 0)),
                  pl.BlockSpec((tb, width), lambda b, t: (b * nt + t, 0)),
                  pl.BlockSpec((tb, GATE_LANES), lambda b, t: (b * nt + t, 0)),
                  pl.BlockSpec((tb, width), lambda b, t: (b * nt + t, 3)),
                  pl.BlockSpec((1, HEAD_DIM), lambda b, t: (0, 0)),
                  pl.BlockSpec((width, D_MODEL), lambda b, t: (0, 0)),
                  pl.BlockSpec((tb, D_MODEL), lambda b, t: (b * nt + t, 0)),
                  pl.BlockSpec((1, D_MODEL), lambda b, t: (0, 0)),
                  pl.BlockSpec((1, D_MODEL), lambda b, t: (0, 0))],
        out_specs=pl.BlockSpec((tb, D_MODEL), lambda b, t: (b * nt + t, 0)),
        out_shape=jax.ShapeDtypeStruct((bt, D_MODEL), F32),
        scratch_shapes=[pltpu.VMEM((HEADS, HEAD_DIM, HEAD_DIM), F32),
                        pltpu.VMEM((HEADS, tb, HEAD_DIM), BF16),
                        pltpu.VMEM((HEADS, tb, HEAD_DIM), F32),
                        pltpu.VMEM((tb, width), BF16),
                        pltpu.VMEM((tb, D_MODEL), F32)],
        compiler_params=_params("parallel", "arbitrary"),
        name="gdn_scan",
    )(wq, u, kdt, attn, egl, proj, norm_w, w_out, x, ln_g, ln_b)


def _gdn_layer(x, w_in, conv_w, a_log, dt_bias, norm_w, w_out, ln_g, ln_b, batch, seq):
    width = HEADS * HEAD_DIM
    w_main = w_in[:, :4 * width].astype(BF16)
    w_gate = jnp.pad(w_in[:, 4 * width:], ((0, 0), (0, GATE_LANES - 2 * HEADS))).astype(BF16)
    per_head = lambda v: jnp.broadcast_to(v.astype(F32)[:, None], (HEADS, GATE_LANES))
    proj, gates = _project(x, [w_main, w_gate], [F32, F32], 512, "gdn_in_proj")
    wq, u, kdt, attn, egl = _gdn_prep(proj, gates, conv_w.astype(F32), per_head(a_log), per_head(dt_bias), batch, seq)
    return _gdn_scan(wq, u, kdt, attn, egl, proj, norm_w.astype(F32)[None, :],
                     w_out.astype(BF16), x, ln_g[None, :], ln_b[None, :], batch, seq)


def _kv_kernel(x_ref, w_ref, kaug_ref, vaug_ref, kmean_ref, *, nq):
    blk = MOBA_BLOCK
    width = HEADS * HEAD_DIM
    aug = 2 * HEAD_DIM
    nblk = x_ref.shape[0] // blk
    lane = lax.broadcasted_iota(jnp.int32, (blk, HEAD_DIM), 1)
    ones_col = (lane == 0).astype(BF16)
    for r in range(nblk):
        rs = slice(r * blk, (r + 1) * blk)
        acc = jnp.dot(x_ref[rs, :].astype(BF16), w_ref[...], preferred_element_type=F32)
        onehot = (lane == lax.rem(pl.program_id(0) * nblk + r, nq)).astype(BF16)
        for h in range(HEADS):
            kaug_ref[rs, h * aug:h * aug + HEAD_DIM] = acc[:, h * HEAD_DIM:(h + 1) * HEAD_DIM].astype(BF16)
            kaug_ref[rs, h * aug + HEAD_DIM:(h + 1) * aug] = onehot
            vaug_ref[rs, h * aug:h * aug + HEAD_DIM] = acc[:, width + h * HEAD_DIM:width + (h + 1) * HEAD_DIM].astype(BF16)
            vaug_ref[rs, h * aug + HEAD_DIM:(h + 1) * aug] = ones_col
        row = lax.rem(pl.program_id(0), 8 // nblk) * nblk + r
        kmean_ref[pl.ds(row, 1), :] = jnp.mean(acc[:, :width], axis=0, keepdims=True)


def _shared_kv(x, w_kv, seq):
    m, k = x.shape
    n = w_kv.shape[1]
    tm = 4 * MOBA_BLOCK
    steps_per_mean_block = 8 * MOBA_BLOCK // tm
    return pl.pallas_call(
        functools.partial(_kv_kernel, nq=seq // MOBA_BLOCK),
        grid=(m // tm,),
        in_specs=[pl.BlockSpec((tm, k), lambda i: (i, 0)),
                  pl.BlockSpec((k, n), lambda i: (0, 0))],
        out_specs=[pl.BlockSpec((tm, n), lambda i: (i, 0)),
                   pl.BlockSpec((tm, n), lambda i: (i, 0)),
                   pl.BlockSpec((8, n // 2), lambda i: (i // steps_per_mean_block, 0))],
        out_shape=[jax.ShapeDtypeStruct((m, n), BF16),
                   jax.ShapeDtypeStruct((m, n), BF16),
                   jax.ShapeDtypeStruct((m // MOBA_BLOCK, n // 2), F32)],
        compiler_params=_params("arbitrary"),
        name="moba_shared_kv",
    )(x, w_kv)


def _moba_select_bias(q_b, kmean, qb):
    nb = kmean.shape[0]
    blk = q_b.shape[0]
    km = jnp.concatenate([kmean, jnp.zeros((HEAD_DIM - nb, HEAD_DIM), F32)], axis=0).astype(BF16)
    gate_t = lax.dot_general(km, q_b, _NT, preferred_element_type=F32)[:nb]
    jrow = lax.broadcasted_iota(jnp.int32, (nb, blk), 0)
    valid = jrow < qb
    jf = jrow.astype(F32)
    g = jnp.where(valid, gate_t, -jnp.inf)
    keep = jnp.zeros((nb, blk), F32)
    for _ in range(MOBA_TOP_K):
        mx = jnp.max(g, axis=0, keepdims=True)
        first = jnp.min(jnp.where(g == mx, jf, float(nb)), axis=0, keepdims=True)
        sel = jf == first
        keep = jnp.where(sel, 1.0, keep)
        g = jnp.where(sel, -jnp.inf, g)
    bias_t = jnp.where(valid & (keep > 0.0), 0.0, MASK_VALUE)
    bias_t = jnp.concatenate([bias_t, jnp.zeros((HEAD_DIM - nb, blk), F32)], axis=0)
    return bias_t.T


def _moba_attn_kernel(q_ref, z_ref, k_ref, v_ref, kmean_ref, o_ref, acc_ref):
    blk = MOBA_BLOCK
    grp = MOBA_GROUP
    aug = 2 * HEAD_DIM
    qb = pl.program_id(2)
    own = pl.multiple_of(qb * blk, blk)
    qi = lax.broadcasted_iota(jnp.int32, (blk, blk), 0)
    ki = lax.broadcasted_iota(jnp.int32, (blk, blk), 1)
    causal = ki <= qi
    zeros_b = jnp.zeros((blk, HEAD_DIM), BF16)

    heads = range(MOBA_HEADS_PER_STEP)
    ha = [slice(h * aug, (h + 1) * aug) for h in heads]
    q_bs = [(q_ref[:, h * HEAD_DIM:(h + 1) * HEAD_DIM] * (HEAD_DIM ** -0.5)).astype(BF16) for h in heads]
    q_augs = [jnp.concatenate([q_bs[h], _moba_select_bias(
        q_bs[h], kmean_ref[:, h * HEAD_DIM:(h + 1) * HEAD_DIM], qb).astype(BF16)], axis=1) for h in heads]
    s_own = [jnp.where(causal, lax.dot_general(jnp.concatenate([q_bs[h], zeros_b], axis=1),
                                               k_ref[pl.ds(own, blk), ha[h]], _NT, preferred_element_type=F32),
                       MASK_VALUE) for h in heads]
    s_rest = [lax.dot_general(q_augs[h], k_ref[0:(grp - 1) * blk, ha[h]], _NT, preferred_element_type=F32)
              for h in heads]
    m_init = [jnp.maximum(jnp.max(s_own[h], axis=-1, keepdims=True), jnp.max(s_rest[h], axis=-1, keepdims=True))
              for h in heads]
    for h in heads:
        acc_ref[h] = (jnp.dot(jnp.exp(s_own[h] - m_init[h]).astype(BF16), v_ref[pl.ds(own, blk), ha[h]],
                              preferred_element_type=F32)
                      + jnp.dot(jnp.exp(s_rest[h] - m_init[h]).astype(BF16), v_ref[0:(grp - 1) * blk, ha[h]],
                                preferred_element_type=F32))

    def group_step(start, ms):
        s = [lax.dot_general(q_augs[h], k_ref[pl.ds(start, grp * blk), ha[h]], _NT, preferred_element_type=F32)
             for h in heads]
        m_new = [jnp.maximum(ms[h], jnp.max(s[h], axis=-1, keepdims=True)) for h in heads]
        p = [jnp.exp(s[h] - m_new[h]).astype(BF16) for h in heads]
        for h in heads:
            acc_ref[h] = jnp.exp(ms[h] - m_new[h]) * acc_ref[h] + jnp.dot(
                p[h], v_ref[pl.ds(start, grp * blk), ha[h]], preferred_element_type=F32)
        return tuple(m_new)

    def group_start(g):
        return pl.multiple_of((grp - 1 + g * grp) * blk, blk)

    n_groups = (jnp.maximum(qb - (grp - 1), 0) + grp - 1) // grp
    ms = lax.fori_loop(0, n_groups // 2,
                       lambda i, ms: group_step(group_start(2 * i + 1), group_step(group_start(2 * i), ms)),
                       tuple(m_init))

    @pl.when(lax.rem(n_groups, 2) == 1)
    def _():
        group_step(group_start(n_groups - 1), ms)

    for h in heads:
        hs = slice(h * HEAD_DIM, (h + 1) * HEAD_DIM)
        acc = acc_ref[h]
        o = acc[:, :HEAD_DIM] / acc[:, HEAD_DIM:HEAD_DIM + 1]
        o_ref[:, hs] = (o * _silu(z_ref[:, hs])).astype(BF16)


def _moba_attn(qz, kaug, vaug, kmean, batch, seq):
    blk = MOBA_BLOCK
    nq = seq // blk
    bt = batch * seq
    hb = MOBA_HEADS_PER_STEP
    groups = HEADS // hb
    assert nq % MOBA_GROUP == 0
    return pl.pallas_call(
        _moba_attn_kernel,
        grid=(batch, groups, nq),
        in_specs=[pl.BlockSpec((blk, hb * HEAD_DIM), lambda b, h, i: (b * nq + i, h)),
                  pl.BlockSpec((blk, hb * HEAD_DIM), lambda b, h, i: (b * nq + i, groups + h)),
                  pl.BlockSpec((seq, hb * 2 * HEAD_DIM), lambda b, h, i: (b, h), pipeline_mode=pl.Buffered(1)),
                  pl.BlockSpec((seq, hb * 2 * HEAD_DIM), lambda b, h, i: (b, h), pipeline_mode=pl.Buffered(1)),
                  pl.BlockSpec((nq, hb * HEAD_DIM), lambda b, h, i: (b, h))],
        out_specs=pl.BlockSpec((blk, hb * HEAD_DIM), lambda b, h, i: (b * nq + i, h)),
        out_shape=jax.ShapeDtypeStruct((bt, HEADS * HEAD_DIM), BF16),
        scratch_shapes=[pltpu.VMEM((hb, blk, 2 * HEAD_DIM), F32)],
        compiler_params=_params("parallel", "parallel", "arbitrary"),
        name="moba_attn",
    )(qz, qz, kaug, vaug, kmean)


def _moba_layer(x, w_in, w_out, ln_g, ln_b, kaug, vaug, kmean, batch, seq):
    (qz,) = _project(x, [w_in.astype(BF16)], [F32], 1024, "moba_in_proj")
    o = _moba_attn(qz, kaug, vaug, kmean, batch, seq)
    return _out_proj_ln(o, w_out.astype(BF16), x, ln_g[None, :], ln_b[None, :], 512, "moba_out_ln")


def kernel(x, a_w_in, a_conv_w, a_A_log, a_dt_bias, a_norm_w, a_w_out, a_ln_g, a_ln_b, b_w_kv, b_w_in, b_w_out, b_ln_g, b_ln_b):
    batch, seq, d_model = x.shape
    assert d_model == D_MODEL and seq % (MOBA_GROUP * MOBA_BLOCK) == 0 and seq // MOBA_BLOCK <= HEAD_DIM
    assert a_w_in.shape[2] == 4 * HEADS * HEAD_DIM + 2 * HEADS and a_conv_w.shape[1] == CONV_K
    h = x.reshape(batch * seq, d_model)
    for i in range(a_w_in.shape[0]):
        h = _gdn_layer(h, a_w_in[i], a_conv_w[i], a_A_log[i], a_dt_bias[i], a_norm_w[i], a_w_out[i],
                       a_ln_g[i], a_ln_b[i], batch, seq)
    kaug, vaug, kmean = _shared_kv(h, b_w_kv.astype(BF16), seq)
    for i in range(b_w_in.shape[0]):
        h = _moba_layer(h, b_w_in[i], b_w_out[i], b_ln_g[i], b_ln_b[i], kaug, vaug, kmean, batch, seq)
    return h.reshape(batch, seq, d_model)
```

```python
import functools

import jax
import jax.numpy as jnp
from jax import lax
from jax.experimental import pallas as pl
from jax.experimental.pallas import tpu as pltpu

F32 = jnp.float32
BF16 = jnp.bfloat16

D_MODEL = 1024
HEADS = 8
HEAD_DIM = 128
CONV_K = 4
GDN_CHUNK = 64
GDN_BLOCK = 256
MOBA_BLOCK = 256
MOBA_TOP_K = 3
MOBA_GROUP = 4
MOBA_HEADS_PER_STEP = 4
DEPTH = 4
DEEPNORM_ALPHA = (2 * DEPTH) ** 0.25
LN_EPS = 1e-5
RMS_EPS = 1e-6
MASK_VALUE = -1e30
GATE_LANES = 128
SUBLANES = 8
LN_ROWS = 64
V7X_VMEM_LIMIT = 56 * 1024 * 1024
PROJ_COL_TILE = 1024
OUT_LN_PARTS = 4

_NT = (((1,), (1,)), ((), ()))


def _params(*sem):
    return pltpu.CompilerParams(dimension_semantics=sem, vmem_limit_bytes=V7X_VMEM_LIMIT)


def _silu(x):
    return x * jax.nn.sigmoid(x)


def _proj_kernel(x_ref, *refs):
    nw = len(refs) // 2
    xb = x_ref[...].astype(BF16)
    for w_ref, o_ref in zip(refs[:nw], refs[nw:]):
        n = w_ref.shape[1]
        tn = min(n, PROJ_COL_TILE)
        for c in range(n // tn):
            o_ref[:, c * tn:(c + 1) * tn] = jnp.dot(xb, w_ref[:, c * tn:(c + 1) * tn],
                                                    preferred_element_type=F32).astype(o_ref.dtype)


def _project(x, weights, widths, out_dtypes, tm, name):
    m, k = x.shape
    return pl.pallas_call(
        _proj_kernel,
        grid=(m // tm,),
        in_specs=[pl.BlockSpec((tm, k), lambda i: (i, 0))]
        + [pl.BlockSpec((k, n), lambda i: (0, 0)) for n in widths],
        out_specs=[pl.BlockSpec((tm, n), lambda i: (i, 0)) for n in widths],
        out_shape=[jax.ShapeDtypeStruct((m, n), dt) for n, dt in zip(widths, out_dtypes)],
        compiler_params=_params("parallel"),
        name=name,
    )(x, *weights)


def _deepnorm_rows(x_ref, y_ref, g_ref, b_ref, out_ref, start, stop):
    rows = LN_ROWS
    for r in range(start // rows, stop // rows):
        rs = slice(r * rows, (r + 1) * rows)
        s = DEEPNORM_ALPHA * x_ref[rs, :] + y_ref[rs, :]
        mu = jnp.mean(s, axis=-1, keepdims=True)
        c = s - mu
        var = jnp.mean(c * c, axis=-1, keepdims=True)
        out_ref[rs, :] = c * lax.rsqrt(var + LN_EPS) * g_ref[...] + b_ref[...]


def _out_ln_kernel(o_ref, w_ref, x_ref, g_ref, b_ref, out_ref, y_ref):
    parts = OUT_LN_PARTS
    size = o_ref.shape[0] // parts

    def matmul_part(i):
        y_ref[i * size:(i + 1) * size, :] = jnp.dot(o_ref[i * size:(i + 1) * size, :], w_ref[...],
                                                    preferred_element_type=F32)

    matmul_part(0)
    for i in range(parts):
        if i + 1 < parts:
            matmul_part(i + 1)
        _deepnorm_rows(x_ref, y_ref, g_ref, b_ref, out_ref, i * size, (i + 1) * size)


def _out_proj_ln(o, w, x, g, b, tm, name):
    m, k = o.shape
    n = w.shape[1]
    return pl.pallas_call(
        _out_ln_kernel,
        grid=(m // tm,),
        in_specs=[pl.BlockSpec((tm, k), lambda i: (i, 0)),
                  pl.BlockSpec((k, n), lambda i: (0, 0)),
                  pl.BlockSpec((tm, n), lambda i: (i, 0)),
                  pl.BlockSpec((1, n), lambda i: (0, 0)),
                  pl.BlockSpec((1, n), lambda i: (0, 0))],
        out_specs=pl.BlockSpec((tm, n), lambda i: (i, 0)),
        out_shape=jax.ShapeDtypeStruct((m, n), F32),
        scratch_shapes=[pltpu.VMEM((tm, n), F32)],
        compiler_params=_params("parallel"),
        name=name,
    )(o, w, x, g, b)


def _gdn_prep_kernel(q_ref, k_ref, v_ref, qp_ref, kp_ref, vp_ref, gates_ref, cw_ref, alog_ref, dtb_ref,
                     wq_ref, u_ref, kdt_ref, attn_ref, egl_ref, xs_ref, p_ref, tm_ref, rhs_ref):
    tb = GDN_BLOCK
    ck = GDN_CHUNK
    nck = tb // ck
    first = pl.program_id(1) == 0
    attn_ref[...] = jnp.zeros_like(attn_ref)

    gate_t = gates_ref[...].T[:2 * HEADS, :]
    wide = lambda r: jnp.concatenate([r[...]] * (tb // GATE_LANES), axis=1)
    beta_t = jax.nn.sigmoid(gate_t[:HEADS, :])
    xg = gate_t[HEADS:, :] + wide(dtb_ref)
    g_t = -jnp.exp(wide(alog_ref)) * (jnp.maximum(xg, 0.0) + jnp.log1p(jnp.exp(-jnp.abs(xg))))
    lin = jnp.bitwise_and(lax.broadcasted_iota(jnp.int32, (HEADS, tb), 1), ck - 1)
    gc_t = g_t
    rest = g_t
    s = 1
    while s < ck:
        gc_t = gc_t + jnp.where(lin >= s, pltpu.roll(gc_t, s, axis=1), 0.0)
        rest = rest + jnp.where(lin < ck - s, pltpu.roll(rest, tb - s, axis=1), 0.0)
        s *= 2
    after = rest - g_t
    cols = jnp.concatenate([beta_t, gc_t, jnp.exp(gc_t), jnp.exp(after), jnp.exp(gc_t + after),
                            jnp.zeros((GATE_LANES - 5 * HEADS, tb), F32)], axis=0).T
    egl_ref[...] = cols

    ri = lax.broadcasted_iota(jnp.int32, (ck, ck), 0)
    ci = lax.broadcasted_iota(jnp.int32, (ck, ck), 1)
    causal = ri >= ci
    strict = ri > ci
    eye = (ri == ci).astype(F32)
    eye_b = eye.astype(BF16)

    def conv_silu(x_ref, p_ref, slot, h, col0):
        sl = slice(h * HEAD_DIM, (h + 1) * HEAD_DIM)
        xs_ref[slot, 0:SUBLANES, :] = jnp.where(first, 0.0, p_ref[:, sl])
        xs_ref[slot, SUBLANES:SUBLANES + tb, :] = x_ref[:, sl]
        cw = cw_ref[:, col0 + h * HEAD_DIM:col0 + (h + 1) * HEAD_DIM]
        acc = xs_ref[slot, SUBLANES:SUBLANES + tb, :] * cw[CONV_K - 1:CONV_K, :]
        for d in range(1, CONV_K):
            acc = acc + xs_ref[slot, SUBLANES - d:SUBLANES - d + tb, :] * cw[CONV_K - 1 - d:CONV_K - d, :]
        return _silu(acc)

    hs = range(HEADS)
    col = lambda group, h: cols[:, group * HEADS + h:group * HEADS + h + 1]
    qc = [conv_silu(q_ref, qp_ref, 3 * h, h, 0) for h in hs]
    kc = [conv_silu(k_ref, kp_ref, 3 * h + 1, h, HEADS * HEAD_DIM) for h in hs]
    vc = [conv_silu(v_ref, vp_ref, 3 * h + 2, h, 2 * HEADS * HEAD_DIM) for h in hs]
    qs = [qc[h] * (lax.rsqrt(jnp.sum(qc[h] * qc[h], axis=-1, keepdims=True) + RMS_EPS) * (HEAD_DIM ** -0.5))
          for h in hs]
    kn = [kc[h] * lax.rsqrt(jnp.sum(kc[h] * kc[h], axis=-1, keepdims=True) + RMS_EPS) for h in hs]
    kb = [kn[h] * col(0, h) for h in hs]
    kn_b = [kn[h].astype(BF16) for h in hs]
    kb_b = [kb[h].astype(BF16) for h in hs]
    qs_b = [qs[h].astype(BF16) for h in hs]
    pairs = [(h, c) for h in hs for c in range(nck)]
    rows = [slice(c * ck, (c + 1) * ck) for c in range(nck)]
    kk = {(h, c): lax.dot_general(kb_b[h][rows[c], :], kn_b[h][rows[c], :], _NT, preferred_element_type=F32)
          for h, c in pairs}
    qk = {(h, c): lax.dot_general(qs_b[h][rows[c], :], kn_b[h][rows[c], :], _NT, preferred_element_type=F32)
          for h, c in pairs}
    for h, c in pairs:
        dm = jnp.exp(jnp.where(causal, col(1, h)[rows[c], :] - gc_t[h:h + 1, rows[c]], -jnp.inf))
        attn_ref[rows[c], h * HEAD_DIM:h * HEAD_DIM + ck] = (qk[h, c] * dm).astype(BF16)
        nm = jnp.where(strict, -(kk[h, c] * dm), 0.0)
        p_ref[h, c] = nm.astype(BF16)
        tm_ref[h, c] = (eye + nm).astype(BF16)
    for h in hs:
        sl = slice(h * HEAD_DIM, (h + 1) * HEAD_DIM)
        rhs_ref[h, :, :HEAD_DIM] = (kb[h] * col(2, h)).astype(BF16)
        rhs_ref[h, :, HEAD_DIM:] = (vc[h] * col(0, h)).astype(BF16)
        qd = (qs[h] * col(2, h)).astype(BF16)
        for c in range(nck):
            wq_ref[(2 * c + 1) * ck:(2 * c + 2) * ck, sl] = qd[rows[c], :]
        kdt_ref[0, sl, :] = (kn[h] * col(3, h)).T.astype(BF16)

    for _ in range(GDN_CHUNK.bit_length() - 2):
        for h, c in pairs:
            p = p_ref[h, c]
            p_ref[h, c] = jnp.dot(p, p, preferred_element_type=F32).astype(BF16)
        for h, c in pairs:
            tm_ref[h, c] = jnp.dot(tm_ref[h, c], p_ref[h, c] + eye_b, preferred_element_type=F32).astype(BF16)
    for h, c in pairs:
        sl = slice(h * HEAD_DIM, (h + 1) * HEAD_DIM)
        rs = slice(c * ck, (c + 1) * ck)
        wu = jnp.dot(tm_ref[h, c], rhs_ref[h, rs, :], preferred_element_type=F32)
        wq_ref[2 * c * ck:(2 * c + 1) * ck, sl] = wu[:, :HEAD_DIM].astype(BF16)
        u_ref[rs, sl] = wu[:, HEAD_DIM:]


def _gdn_prep(proj, gates, conv_w, alog_row, dtb_row, batch, seq):
    tb = GDN_BLOCK
    nt = seq // tb
    bt = batch * seq
    width = HEADS * HEAD_DIM
    blk = lambda col: pl.BlockSpec((tb, width), lambda b, t: (b * nt + t, col))
    prev = lambda col: pl.BlockSpec((SUBLANES, width),
                                    lambda b, t: (jnp.maximum((b * nt + t) * (tb // SUBLANES) - 1, 0), col))
    return pl.pallas_call(
        _gdn_prep_kernel,
        grid=(batch, nt),
        in_specs=[blk(0), blk(1), blk(2), prev(0), prev(1), prev(2),
                  pl.BlockSpec((tb, GATE_LANES), lambda b, t: (b * nt + t, 0)),
                  pl.BlockSpec((CONV_K, 3 * width), lambda b, t: (0, 0)),
                  pl.BlockSpec((HEADS, GATE_LANES), lambda b, t: (0, 0)),
                  pl.BlockSpec((HEADS, GATE_LANES), lambda b, t: (0, 0))],
        out_specs=[pl.BlockSpec((2 * tb, width), lambda b, t: (b * nt + t, 0)),
                   pl.BlockSpec((tb, width), lambda b, t: (b * nt + t, 0)),
                   pl.BlockSpec((1, width, tb), lambda b, t: (b * nt + t, 0, 0)),
                   pl.BlockSpec((tb, width), lambda b, t: (b * nt + t, 0)),
                   pl.BlockSpec((tb, GATE_LANES), lambda b, t: (b * nt + t, 0))],
        out_shape=[jax.ShapeDtypeStruct((2 * bt, width), BF16),
                   jax.ShapeDtypeStruct((bt, width), F32),
                   jax.ShapeDtypeStruct((bt // tb, width, tb), BF16),
                   jax.ShapeDtypeStruct((bt, width), BF16),
                   jax.ShapeDtypeStruct((bt, GATE_LANES), F32)],
        scratch_shapes=[pltpu.VMEM((3 * HEADS, tb + SUBLANES, HEAD_DIM), F32),
                        pltpu.VMEM((HEADS, tb // GDN_CHUNK, GDN_CHUNK, GDN_CHUNK), BF16),
                        pltpu.VMEM((HEADS, tb // GDN_CHUNK, GDN_CHUNK, GDN_CHUNK), BF16),
                        pltpu.VMEM((HEADS, tb, 2 * HEAD_DIM), BF16)],
        compiler_params=_params("parallel", "arbitrary"),
        name="gdn_prep",
    )(proj, proj, proj, proj, proj, proj, gates, conv_w, alog_row, dtb_row)


def _gdn_scan_kernel(wq_ref, u_ref, kdt_ref, attn_ref, egl_ref, z_ref, nw_ref, wo_ref, x_ref, g_ref, b_ref,
                     out_ref, s_ref, vn_ref, qs_ref, o_ref, y_ref):
    tb = GDN_BLOCK
    nck = tb // GDN_CHUNK
    ck = GDN_CHUNK

    @pl.when(pl.program_id(1) == 0)
    def _():
        s_ref[...] = jnp.zeros_like(s_ref)

    zeros_half = jnp.zeros((ck, HEAD_DIM), BF16)
    for c in range(nck):
        rs = slice(c * ck, (c + 1) * ck)
        for h in range(HEADS):
            sl = slice(h * HEAD_DIM, (h + 1) * HEAD_DIM)
            state = s_ref[h]
            r = jnp.dot(wq_ref[2 * c * ck:(2 * c + 2) * ck, sl], state.astype(BF16),
                        preferred_element_type=F32)
            vnew = (u_ref[rs, sl] - r[:ck, :]).astype(BF16)
            qs_ref[h, rs, :] = r[ck:, :]
            vn_ref[h, rs, :] = vnew
            vpad = jnp.concatenate([vnew, zeros_half] if c % 2 == 0 else [zeros_half, vnew], axis=0)
            slab = kdt_ref[0, sl, (c // 2) * 2 * ck:(c // 2 + 1) * 2 * ck]
            cd = egl_ref[c * ck:c * ck + 1, 4 * HEADS + h:4 * HEADS + h + 1]
            s_ref[h] = state * cd + jnp.dot(slab, vpad, preferred_element_type=F32)
    for h in range(HEADS):
        sl = slice(h * HEAD_DIM, (h + 1) * HEAD_DIM)
        for c in range(nck):
            rs = slice(c * ck, (c + 1) * ck)
            o = qs_ref[h, rs, :] + jnp.dot(attn_ref[rs, h * HEAD_DIM:h * HEAD_DIM + ck], vn_ref[h, rs, :],
                                           preferred_element_type=F32)
            o = o * lax.rsqrt(jnp.mean(o * o, axis=-1, keepdims=True) + RMS_EPS) * nw_ref[...]
            o_ref[rs, sl] = (o * _silu(z_ref[rs, sl])).astype(BF16)
    y_ref[...] = jnp.dot(o_ref[...], wo_ref[...], preferred_element_type=F32)
    _deepnorm_rows(x_ref, y_ref, g_ref, b_ref, out_ref, 0, tb)


def _gdn_scan(wq, u, kdt, attn, egl, proj, norm_w, w_out, x, ln_g, ln_b, batch, seq):
    tb = GDN_BLOCK
    nt = seq // tb
    bt = batch * seq
    width = HEADS * HEAD_DIM
    return pl.pallas_call(
        _gdn_scan_kernel,
        grid=(batch, nt),
        in_specs=[pl.BlockSpec((2 * tb, width), lambda b, t: (b * nt + t, 0)),
                  pl.BlockSpec((tb, width), lambda b, t: (b * nt + t, 0)),
                  pl.BlockSpec((1, width, tb), lambda b, t: (b * nt + t, 0, 0)),
                  pl.BlockSpec((tb, width), lambda b, t: (b * nt + t, 0)),
                  pl.BlockSpec((tb, GATE_LANES), lambda b, t: (b * nt + t, 0)),
                  pl.BlockSpec((tb, width), lambda b, t: (b * nt + t, 3)),
                  pl.BlockSpec((1, HEAD_DIM), lambda b, t: (0, 0)),
                  pl.BlockSpec((width, D_MODEL), lambda b, t: (0, 0)),
                  pl.BlockSpec((tb, D_MODEL), lambda b, t: (b * nt + t, 0)),
                  pl.BlockSpec((1, D_MODEL), lambda b, t: (0, 0)),
                  pl.BlockSpec((1, D_MODEL), lambda b, t: (0, 0))],
        out_specs=pl.BlockSpec((tb, D_MODEL), lambda b, t: (b * nt + t, 0)),
        out_shape=jax.ShapeDtypeStruct((bt, D_MODEL), F32),
        scratch_shapes=[pltpu.VMEM((HEADS, HEAD_DIM, HEAD_DIM), F32),
                        pltpu.VMEM((HEADS, tb, HEAD_DIM), BF16),
                        pltpu.VMEM((HEADS, tb, HEAD_DIM), F32),
                        pltpu.VMEM((tb, width), BF16),
                        pltpu.VMEM((tb, D_MODEL), F32)],
        compiler_params=_params("parallel", "arbitrary"),
        name="gdn_scan",
    )(wq, u, kdt, attn, egl, proj, norm_w, w_out, x, ln_g, ln_b)


def _gdn_layer(x, w_in, conv_w, a_log, dt_bias, norm_w, w_out, ln_g, ln_b, batch, seq):
    width = HEADS * HEAD_DIM
    w_gate = jnp.pad(w_in[:, 4 * width:], ((0, 0), (0, GATE_LANES - 2 * HEADS))).astype(BF16)
    per_head = lambda v: jnp.broadcast_to(v.astype(F32)[:, None], (HEADS, GATE_LANES))
    proj, gates = _project(x, [w_in.astype(BF16), w_gate], [4 * width, GATE_LANES], [F32, F32], 512, "gdn_in_proj")
    wq, u, kdt, attn, egl = _gdn_prep(proj, gates, conv_w.astype(F32), per_head(a_log), per_head(dt_bias), batch, seq)
    return _gdn_scan(wq, u, kdt, attn, egl, proj, norm_w.astype(F32)[None, :],
                     w_out.astype(BF16), x, ln_g[None, :], ln_b[None, :], batch, seq)


def _kv_kernel(x_ref, w_ref, kaug_ref, vaug_ref, kmean_ref, *, nq):
    blk = MOBA_BLOCK
    width = HEADS * HEAD_DIM
    aug = 2 * HEAD_DIM
    nblk = x_ref.shape[0] // blk
    lane = lax.broadcasted_iota(jnp.int32, (blk, HEAD_DIM), 1)
    ones_col = (lane == 0).astype(BF16)
    for r in range(nblk):
        rs = slice(r * blk, (r + 1) * blk)
        acc = jnp.dot(x_ref[rs, :].astype(BF16), w_ref[...], preferred_element_type=F32)
        onehot = (lane == lax.rem(pl.program_id(0) * nblk + r, nq)).astype(BF16)
        for h in range(HEADS):
            kaug_ref[rs, h * aug:h * aug + HEAD_DIM] = acc[:, h * HEAD_DIM:(h + 1) * HEAD_DIM].astype(BF16)
            kaug_ref[rs, h * aug + HEAD_DIM:(h + 1) * aug] = onehot
            vaug_ref[rs, h * aug:h * aug + HEAD_DIM] = acc[:, width + h * HEAD_DIM:width + (h + 1) * HEAD_DIM].astype(BF16)
            vaug_ref[rs, h * aug + HEAD_DIM:(h + 1) * aug] = ones_col
        row = lax.rem(pl.program_id(0), SUBLANES // nblk) * nblk + r
        kmean_ref[pl.ds(row, 1), :] = jnp.mean(acc[:, :width], axis=0, keepdims=True)


def _shared_kv(x, w_kv, seq):
    m, k = x.shape
    n = w_kv.shape[1]
    tm = 4 * MOBA_BLOCK
    steps_per_mean_block = SUBLANES * MOBA_BLOCK // tm
    return pl.pallas_call(
        functools.partial(_kv_kernel, nq=seq // MOBA_BLOCK),
        grid=(m // tm,),
        in_specs=[pl.BlockSpec((tm, k), lambda i: (i, 0)),
                  pl.BlockSpec((k, n), lambda i: (0, 0))],
        out_specs=[pl.BlockSpec((tm, n), lambda i: (i, 0)),
                   pl.BlockSpec((tm, n), lambda i: (i, 0)),
                   pl.BlockSpec((SUBLANES, n // 2), lambda i: (i // steps_per_mean_block, 0))],
        out_shape=[jax.ShapeDtypeStruct((m, n), BF16),
                   jax.ShapeDtypeStruct((m, n), BF16),
                   jax.ShapeDtypeStruct((m // MOBA_BLOCK, n // 2), F32)],
        compiler_params=_params("arbitrary"),
        name="moba_shared_kv",
    )(x, w_kv)


def _moba_select_bias(q_b, kmean, qb):
    nb = kmean.shape[0]
    blk = q_b.shape[0]
    km = jnp.concatenate([kmean, jnp.zeros((HEAD_DIM - nb, HEAD_DIM), F32)], axis=0).astype(BF16)
    gate_t = lax.dot_general(km, q_b, _NT, preferred_element_type=F32)[:nb]
    jrow = lax.broadcasted_iota(jnp.int32, (nb, blk), 0)
    valid = jrow < qb
    jf = jrow.astype(F32)
    g = jnp.where(valid, gate_t, -jnp.inf)
    keep = jnp.zeros((nb, blk), F32)
    for _ in range(MOBA_TOP_K):
        mx = jnp.max(g, axis=0, keepdims=True)
        first = jnp.min(jnp.where(g == mx, jf, float(nb)), axis=0, keepdims=True)
        sel = jf == first
        keep = jnp.where(sel, 1.0, keep)
        g = jnp.where(sel, -jnp.inf, g)
    bias_t = jnp.where(valid & (keep > 0.0), 0.0, MASK_VALUE)
    bias_t = jnp.concatenate([bias_t, jnp.zeros((HEAD_DIM - nb, blk), F32)], axis=0)
    return bias_t.T


def _moba_attn_kernel(q_ref, z_ref, k_ref, v_ref, kmean_ref, o_ref, acc_ref):
    blk = MOBA_BLOCK
    grp = MOBA_GROUP
    aug = 2 * HEAD_DIM
    qb = pl.program_id(2)
    own = pl.multiple_of(qb * blk, blk)
    qi = lax.broadcasted_iota(jnp.int32, (blk, blk), 0)
    ki = lax.broadcasted_iota(jnp.int32, (blk, blk), 1)
    causal = ki <= qi
    zeros_b = jnp.zeros((blk, HEAD_DIM), BF16)

    heads = range(MOBA_HEADS_PER_STEP)
    ha = [slice(h * aug, (h + 1) * aug) for h in heads]
    q_bs = [(q_ref[:, h * HEAD_DIM:(h + 1) * HEAD_DIM] * (HEAD_DIM ** -0.5)).astype(BF16) for h in heads]
    q_augs = [jnp.concatenate([q_bs[h], _moba_select_bias(
        q_bs[h], kmean_ref[:, h * HEAD_DIM:(h + 1) * HEAD_DIM], qb).astype(BF16)], axis=1) for h in heads]
    s_own = [jnp.where(causal, lax.dot_general(jnp.concatenate([q_bs[h], zeros_b], axis=1),
                                               k_ref[pl.ds(own, blk), ha[h]], _NT, preferred_element_type=F32),
                       MASK_VALUE) for h in heads]
    s_rest = [lax.dot_general(q_augs[h], k_ref[0:(grp - 1) * blk, ha[h]], _NT, preferred_element_type=F32)
              for h in heads]
    m_init = [jnp.maximum(jnp.max(s_own[h], axis=-1, keepdims=True), jnp.max(s_rest[h], axis=-1, keepdims=True))
              for h in heads]
    for h in heads:
        acc_ref[h] = (jnp.dot(jnp.exp(s_own[h] - m_init[h]).astype(BF16), v_ref[pl.ds(own, blk), ha[h]],
                              preferred_element_type=F32)
                      + jnp.dot(jnp.exp(s_rest[h] - m_init[h]).astype(BF16), v_ref[0:(grp - 1) * blk, ha[h]],
                                preferred_element_type=F32))

    def group_step(start, ms):
        s = [lax.dot_general(q_augs[h], k_ref[pl.ds(start, grp * blk), ha[h]], _NT, preferred_element_type=F32)
             for h in heads]
        m_new = [jnp.maximum(ms[h], jnp.max(s[h], axis=-1, keepdims=True)) for h in heads]
        p = [jnp.exp(s[h] - m_new[h]).astype(BF16) for h in heads]
        for h in heads:
            acc_ref[h] = jnp.exp(ms[h] - m_new[h]) * acc_ref[h] + jnp.dot(
                p[h], v_ref[pl.ds(start, grp * blk), ha[h]], preferred_element_type=F32)
        return tuple(m_new)

    def group_start(g):
        return pl.multiple_of((grp - 1 + g * grp) * blk, blk)

    n_groups = (jnp.maximum(qb - (grp - 1), 0) + grp - 1) // grp
    ms = lax.fori_loop(0, n_groups // 2,
                       lambda i, ms: group_step(group_start(2 * i + 1), group_step(group_start(2 * i), ms)),
                       tuple(m_init))

    @pl.when(lax.rem(n_groups, 2) == 1)
    def _():
        group_step(group_start(n_groups - 1), ms)

    for h in heads:
        hs = slice(h * HEAD_DIM, (h + 1) * HEAD_DIM)
        acc = acc_ref[h]
        o = acc[:, :HEAD_DIM] / acc[:, HEAD_DIM:HEAD_DIM + 1]
        o_ref[:, hs] = (o * _silu(z_ref[:, hs])).astype(BF16)


def _moba_attn(qz, kaug, vaug, kmean, batch, seq):
    blk = MOBA_BLOCK
    nq = seq // blk
    bt = batch * seq
    hb = MOBA_HEADS_PER_STEP
    groups = HEADS // hb
    assert nq % MOBA_GROUP == 0
    return pl.pallas_call(
        _moba_attn_kernel,
        grid=(batch, groups, nq),
        in_specs=[pl.BlockSpec((blk, hb * HEAD_DIM), lambda b, h, i: (b * nq + i, h)),
                  pl.BlockSpec((blk, hb * HEAD_DIM), lambda b, h, i: (b * nq + i, groups + h)),
                  pl.BlockSpec((seq, hb * 2 * HEAD_DIM), lambda b, h, i: (b, h), pipeline_mode=pl.Buffered(1)),
                  pl.BlockSpec((seq, hb * 2 * HEAD_DIM), lambda b, h, i: (b, h), pipeline_mode=pl.Buffered(1)),
                  pl.BlockSpec((nq, hb * HEAD_DIM), lambda b, h, i: (b, h))],
        out_specs=pl.BlockSpec((blk, hb * HEAD_DIM), lambda b, h, i: (b * nq + i, h)),
        out_shape=jax.ShapeDtypeStruct((bt, HEADS * HEAD_DIM), BF16),
        scratch_shapes=[pltpu.VMEM((hb, blk, 2 * HEAD_DIM), F32)],
        compiler_params=_params("parallel", "parallel", "arbitrary"),
        name="moba_attn",
    )(qz, qz, kaug, vaug, kmean)


def _moba_layer(x, w_in, w_out, ln_g, ln_b, kaug, vaug, kmean, batch, seq):
    (qz,) = _project(x, [w_in.astype(BF16)], [w_in.shape[1]], [F32], 1024, "moba_in_proj")
    o = _moba_attn(qz, kaug, vaug, kmean, batch, seq)
    return _out_proj_ln(o, w_out.astype(BF16), x, ln_g[None, :], ln_b[None, :], 1024, "moba_out_ln")


def kernel(x, a_w_in, a_conv_w, a_A_log, a_dt_bias, a_norm_w, a_w_out, a_ln_g, a_ln_b, b_w_kv, b_w_in, b_w_out, b_ln_g, b_ln_b):
    batch, seq, d_model = x.shape
    assert d_model == D_MODEL and seq % (MOBA_GROUP * MOBA_BLOCK) == 0 and seq // MOBA_BLOCK <= HEAD_DIM
    assert a_w_in.shape[2] == 4 * HEADS * HEAD_DIM + 2 * HEADS and a_conv_w.shape[1] == CONV_K
    h = x.reshape(batch * seq, d_model)
    for i in range(a_w_in.shape[0]):
        h = _gdn_layer(h, a_w_in[i], a_conv_w[i], a_A_log[i], a_dt_bias[i], a_norm_w[i], a_w_out[i],
                       a_ln_g[i], a_ln_b[i], batch, seq)
    kaug, vaug, kmean = _shared_kv(h, b_w_kv.astype(BF16), seq)
    for i in range(b_w_in.shape[0]):
        h = _moba_layer(h, b_w_in[i], b_w_out[i], b_ln_g[i], b_ln_b[i], kaug, vaug, kmean, batch, seq)
    return h.reshape(batch, seq, d_model)
```
